```python
import jax, jax.numpy as jnp
from jax import lax
import numpy as np

D_MODEL = 1024
BATCH = 8
SEQ = 8192
DEPTH = 1

HG_HEADS = 8
HG_DIM = 128
HG_WIDTH = HG_HEADS * HG_DIM
HG_CHUNK = 32
MLA_HEADS = 8
QK_NOPE = 128
QK_ROPE = 64
QK_DIM = QK_NOPE + QK_ROPE
V_DIM = 128
Q_LORA = 3 * D_MODEL // 8
KV_LORA = D_MODEL // 4
MLA_WIDTH = MLA_HEADS * V_DIM
Q_BLOCK = 128
ROPE_THETA = 10000.0
N_BRANCH = 2
EPS = 1e-6
IN_SPLITS = (HG_WIDTH, HG_WIDTH, HG_WIDTH, HG_WIDTH,
             Q_LORA, KV_LORA, QK_ROPE, MLA_WIDTH,
             N_BRANCH * D_MODEL)
IN_COLS = sum(IN_SPLITS)

kernel_name = "hgrn2_mla_gated_parallel_hybrid"


def _split_points():
    pts, acc = [], 0
    for w in IN_SPLITS[:-1]:
        acc += w
        pts.append(acc)
    return pts


def rms_norm(x, g):
    xf = x.astype(jnp.float32)
    y = xf * lax.rsqrt(jnp.mean(xf * xf, axis=-1, keepdims=True) + EPS)
    return (y * g.astype(jnp.float32)).astype(x.dtype)


def forget_lower_bounds(lb_logits):
    return jnp.cumsum(jax.nn.softmax(lb_logits.astype(jnp.float32), axis=0), axis=0)[:DEPTH]


def rope_tables(seq):
    inv = ROPE_THETA ** (-jnp.arange(0, QK_ROPE, 2, dtype=jnp.float32) / QK_ROPE)
    ang = jnp.arange(seq, dtype=jnp.float32)[:, None] * inv[None, :]
    return jnp.cos(ang), jnp.sin(ang)


def apply_rope(x, cos, sin):
    xf = x.astype(jnp.float32)
    x1, x2 = xf[..., : QK_ROPE // 2], xf[..., QK_ROPE // 2:]
    out = jnp.concatenate([x1 * cos - x2 * sin, x2 * cos + x1 * sin], axis=-1)
    return out.astype(x.dtype)


def hgrn2_recurrence(q, k, v, log_f):
    B, S, H, dk = q.shape
    dv = v.shape[-1]
    C = HG_CHUNK
    N = S // C

    def to_chunks(t):
        return t.reshape(B, N, C, H, t.shape[-1]).transpose(1, 0, 3, 2, 4)

    q, k, v, log_f = to_chunks(q), to_chunks(k), to_chunks(v), to_chunks(log_f)
    b = jnp.cumsum(log_f, axis=3)
    b_last = b[:, :, :, -1:, :]
    q_in = q * jnp.exp(b)
    k_in = k * jnp.exp(-b)
    k_out = k * jnp.exp(b_last - b)
    chunk_decay = jnp.exp(b_last[:, :, :, 0, :])

    causal = jnp.tril(jnp.ones((C, C), dtype=bool))
    scores = jnp.einsum('nbhtk,nbhsk->nbhts', q_in, k_in)
    scores = jnp.where(causal, scores, 0.0)
    o_intra = jnp.einsum('nbhts,nbhsv->nbhtv', scores, v)

    def step(state, inp):
        q_n, k_n, v_n, dec_n = inp
        o_inter = jnp.einsum('bhtk,bhkv->bhtv', q_n, state)
        state = state * dec_n[..., None] + jnp.einsum('bhsk,bhsv->bhkv', k_n, v_n)
        return state, o_inter

    state0 = jnp.zeros((B, H, dk, dv), jnp.float32)
    _, o_inter = lax.scan(step, state0, (q_in, k_out, v, chunk_decay))
    o = o_intra + o_inter
    return o.transpose(1, 0, 3, 2, 4).reshape(B, S, H, dv)


def hgrn2_branch(hq, hf, hi, hz, lb, hg_norm_g):
    B, S, _ = hq.shape
    dt = hq.dtype
    f = lb + (1.0 - lb) * jax.nn.sigmoid(hf.astype(jnp.float32))
    q = jax.nn.silu(hq.astype(jnp.float32)).reshape(B, S, HG_HEADS, HG_DIM)
    k = (1.0 - f).reshape(B, S, HG_HEADS, HG_DIM)
    v = hi.astype(jnp.float32).reshape(B, S, HG_HEADS, HG_DIM)
    log_f = jnp.log(f).reshape(B, S, HG_HEADS, HG_DIM)
    o = hgrn2_recurrence(q, k, v, log_f).astype(dt)
    o = rms_norm(o, hg_norm_g)
    o = o * jax.nn.silu(hz).reshape(B, S, HG_HEADS, HG_DIM)
    return o.reshape(B, S, HG_WIDTH)


def mla_branch(cq, ckv, kr, mz, q_a_g, w_uq, kv_a_g, w_ukv):
    B, S, _ = cq.shape
    cos, sin = rope_tables(S)
    q = (rms_norm(cq, q_a_g) @ w_uq).reshape(B, S, MLA_HEADS, QK_DIM)
    q_nope = q[..., :QK_NOPE]
    q_pe = apply_rope(q[..., QK_NOPE:], cos[:, None, :], sin[:, None, :])
    kv = (rms_norm(ckv, kv_a_g) @ w_ukv).reshape(B, S, MLA_HEADS, QK_NOPE + V_DIM)
    k_nope, v = kv[..., :QK_NOPE], kv[..., QK_NOPE:]
    k_pe = apply_rope(kr, cos, sin)
    scale = QK_DIM ** -0.5
    key_pos = jnp.arange(S)

    def attend_block(blk):
        start = blk * Q_BLOCK
        qn = lax.dynamic_slice_in_dim(q_nope, start, Q_BLOCK, axis=1)
        qp = lax.dynamic_slice_in_dim(q_pe, start, Q_BLOCK, axis=1)
        s = (jnp.einsum('bqhd,bkhd->bhqk', qn, k_nope)
             + jnp.einsum('bqhr,bkr->bhqk', qp, k_pe)).astype(jnp.float32) * scale
        q_pos = start + jnp.arange(Q_BLOCK)
        s = jnp.where(q_pos[:, None] >= key_pos[None, :], s, -jnp.inf)
        p = jax.nn.softmax(s, axis=-1).astype(v.dtype)
        return jnp.einsum('bhqk,bkhd->bqhd', p, v)

    out = lax.map(attend_block, jnp.arange(S // Q_BLOCK))
    out = out.transpose(1, 0, 2, 3, 4).reshape(B, S, MLA_WIDTH)
    return out * jax.nn.silu(mz)


def setup_inputs(seed: int = 0) -> dict:
    key = jax.random.key(seed)
    ks = jax.random.split(key, 16)
    f32 = jnp.float32

    def w(k, shape, fan_in):
        return jax.random.normal(k, shape, f32) * fan_in ** -0.5

    def gain(k, shape):
        return 1.0 + 0.02 * jax.random.normal(k, shape, f32)

    return {
        "x": jax.random.normal(ks[0], (BATCH, SEQ, D_MODEL), f32),
        "norm_g": gain(ks[1], (DEPTH, D_MODEL)),
        "w_in": w(ks[2], (DEPTH, D_MODEL, IN_COLS), D_MODEL),
        "b_gate": 0.02 * jax.random.normal(ks[3], (DEPTH, N_BRANCH * D_MODEL), f32),
        "lb_logits": 0.1 * jax.random.normal(ks[4], (DEPTH + 1, HG_WIDTH), f32),
        "hg_norm_g": gain(ks[5], (DEPTH, HG_DIM)),
        "q_a_g": gain(ks[6], (DEPTH, Q_LORA)),
        "w_uq": w(ks[7], (DEPTH, Q_LORA, MLA_HEADS * QK_DIM), Q_LORA),
        "kv_a_g": gain(ks[8], (DEPTH, KV_LORA)),
        "w_ukv": w(ks[9], (DEPTH, KV_LORA, MLA_HEADS * (QK_NOPE + V_DIM)), KV_LORA),
        "w_proj_a": w(ks[10], (DEPTH, HG_WIDTH, D_MODEL), HG_WIDTH),
        "w_proj_b": w(ks[11], (DEPTH, MLA_WIDTH, D_MODEL), MLA_WIDTH),
        "w_out": w(ks[12], (DEPTH, D_MODEL, D_MODEL), D_MODEL),
        "final_norm_g": gain(ks[13], (D_MODEL,)),
    }


def reference(x, norm_g, w_in, b_gate, lb_logits, hg_norm_g, q_a_g, w_uq, kv_a_g, w_ukv,
              w_proj_a, w_proj_b, w_out, final_norm_g):
    B, S, _ = x.shape
    lower_bounds = forget_lower_bounds(lb_logits)
    pts = _split_points()
    for l in range(DEPTH):
        h = rms_norm(x, norm_g[l])
        proj = h @ w_in[l]
        hq, hf, hi, hz, cq, ckv, kr, mz, glog = jnp.split(proj, pts, axis=-1)
        y_a = hgrn2_branch(hq, hf, hi, hz, lower_bounds[l], hg_norm_g[l])
        y_b = mla_branch(cq, ckv, kr, mz, q_a_g[l], w_uq[l], kv_a_g[l], w_ukv[l])
        gates = jax.nn.sigmoid((glog + b_gate[l]).astype(jnp.float32)).astype(x.dtype)
        gates = gates.reshape(B, S, N_BRANCH, D_MODEL)
        merged = gates[:, :, 0] * (y_a @ w_proj_a[l]) + gates[:, :, 1] * (y_b @ w_proj_b[l])
        x = x + merged @ w_out[l]
    return rms_norm(x, final_norm_g)
```

```python
import functools
import math

import jax
import jax.numpy as jnp
from jax import lax
from jax.experimental import pallas as pl
from jax.experimental.pallas import tpu as pltpu

F32 = jnp.float32
BF16 = jnp.bfloat16

D_MODEL = 1024
HG_HEADS = 8
HG_DIM = 128
HG_WIDTH = HG_HEADS * HG_DIM
HG_CHUNK = 32
MLA_HEADS = 8
QK_NOPE = 128
QK_ROPE = 64
QK_DIM = QK_NOPE + QK_ROPE
V_DIM = 128
Q_LORA = 3 * D_MODEL // 8
KV_LORA = D_MODEL // 4
MLA_WIDTH = MLA_HEADS * V_DIM
ROPE_THETA = 10000.0
EPS = 1e-6

LANES = 128
QK_PAD = 2 * LANES
HG_GROUP = 128
VMEM_LIMIT = 56 * 1024 * 1024

PROJ_TM = 512
PROJ_CW = 512
HG_TC = 512
MLA_TS = 512
ATT_TQ = 512
ATT_TK = 512
MERGE_TM = 512

MAIN_W = 6 * D_MODEL
MLA_W = Q_LORA + KV_LORA + 2 * LANES


def _resident(shape):
    return pl.BlockSpec(shape, lambda *_: (0,) * len(shape), pipeline_mode=pl.Buffered(1))


def _rms(x, g):
    return x * lax.rsqrt(jnp.mean(x * x, axis=-1, keepdims=True) + EPS) * g


def _dot(a, b):
    return jnp.dot(a, b, preferred_element_type=F32)


def _dot_nt(a, b):
    return lax.dot_general(a, b, (((1,), (1,)), ((), ())), preferred_element_type=F32)


def _proj_body(x_ref, g_ref, wmain_ref, whf_ref, wmla_ref, main_ref, hf_ref, mla_ref):
    h = _rms(x_ref[...], g_ref[...]).astype(BF16)
    for c in range(MAIN_W // PROJ_CW):
        cs = slice(c * PROJ_CW, (c + 1) * PROJ_CW)
        main_ref[:, cs] = _dot(h, wmain_ref[:, cs]).astype(BF16)
    for c in range(HG_WIDTH // PROJ_CW):
        cs = slice(c * PROJ_CW, (c + 1) * PROJ_CW)
        hf_ref[:, cs] = _dot(h, whf_ref[:, cs])
    mla_ref[...] = _dot(h, wmla_ref[...]).astype(BF16)


def _proj(x2, norm_g, w_main, w_hf, w_mla):
    t = x2.shape[0]
    row = lambda w: pl.BlockSpec((PROJ_TM, w), lambda i: (i, 0))
    return pl.pallas_call(
        _proj_body,
        grid=(t // PROJ_TM,),
        in_specs=[row(D_MODEL), _resident((1, D_MODEL)), _resident(w_main.shape),
                  _resident(w_hf.shape), _resident(w_mla.shape)],
        out_specs=[row(MAIN_W), row(HG_WIDTH), row(MLA_W)],
        out_shape=[jax.ShapeDtypeStruct((t, MAIN_W), BF16),
                   jax.ShapeDtypeStruct((t, HG_WIDTH), F32),
                   jax.ShapeDtypeStruct((t, MLA_W), BF16)],
        compiler_params=pltpu.CompilerParams(
            dimension_semantics=("arbitrary",), vmem_limit_bytes=VMEM_LIMIT),
        name="proj",
    )(x2, norm_g, w_main, w_hf, w_mla)


def _hgrn_body(hq_ref, hf_ref, hi_ref, hz_ref, lb_ref, g_ref, o_ref, st_ref):
    @pl.when(pl.program_id(1) == 0)
    def _():
        st_ref[...] = jnp.zeros_like(st_ref)

    r = lax.broadcasted_iota(jnp.int32, (HG_GROUP, HG_GROUP), 0)
    c = lax.broadcasted_iota(jnp.int32, (HG_GROUP, HG_GROUP), 1)
    shift = HG_CHUNK.bit_length() - 1
    chunk_r = lax.shift_right_logical(r, shift)
    same_chunk = chunk_r == lax.shift_right_logical(c, shift)
    causal = same_chunk & (r >= c)
    sum_mat = jnp.concatenate([jnp.where(causal, 1.0, 0.0), jnp.where(same_chunk, 1.0, 0.0)],
                              axis=0).astype(BF16)
    n_chunks = HG_GROUP // HG_CHUNK

    def group(gi, carry):
        rows = pl.ds(pl.multiple_of(gi * HG_GROUP, HG_GROUP), HG_GROUP)
        for h in range(HG_HEADS):
            cols = slice(h * HG_DIM, (h + 1) * HG_DIM)
            lb = lb_ref[:, cols]
            f = lb + (1.0 - lb) * jax.nn.sigmoid(hf_ref[rows, cols])
            log_f = jnp.log(f)
            k = 1.0 - f
            lf_hi = log_f.astype(BF16)
            lf_lo = (log_f - lf_hi.astype(F32)).astype(BF16)
            sums = _dot(sum_mat, lf_hi) + _dot(sum_mat, lf_lo)
            b = sums[:HG_GROUP]
            b_last = sums[HG_GROUP:]
            hq = hq_ref[rows, cols].astype(F32)
            q_in = (hq * jax.nn.sigmoid(hq) * jnp.exp(b)).astype(BF16)
            k_in = (k * jnp.exp(-b)).astype(BF16)
            k_out = k * jnp.exp(b_last - b)
            decay = jnp.exp(b_last)
            v = hi_ref[rows, cols]
            v_t = v.astype(F32).T.astype(BF16)

            scores = jnp.where(causal, _dot_nt(q_in, k_in), 0.0).astype(BF16)
            o_intra = _dot(scores, v)

            st = st_ref[h]
            o_inter = []
            for ci in range(n_chunks):
                cr = slice(ci * HG_CHUNK, (ci + 1) * HG_CHUNK)
                o_inter.append(_dot_nt(q_in[cr], st.astype(BF16)))
                k_out_c = jnp.where(chunk_r == ci, k_out, 0.0).astype(BF16)
                st = st * decay[ci * HG_CHUNK:ci * HG_CHUNK + 1] + _dot(v_t, k_out_c)
            st_ref[h] = st
            o = o_intra + jnp.concatenate(o_inter, axis=0)

            hz = hz_ref[rows, cols].astype(F32)
            y = _rms(o, g_ref[...]) * (hz * jax.nn.sigmoid(hz))
            o_ref[rows, cols] = y.astype(BF16)
        return carry

    lax.fori_loop(0, HG_TC // HG_GROUP, group, 0)


def _hgrn(main, hf, lb, hg_g, batch, seq):
    nblk = seq // HG_TC
    t = batch * seq

    def col(j):
        return pl.BlockSpec((HG_TC, HG_WIDTH), lambda b, s: (b * nblk + s, j))

    return pl.pallas_call(
        _hgrn_body,
        grid=(batch, nblk),
        in_specs=[col(0), col(0), col(1), col(2),
                  _resident((1, HG_WIDTH)), _resident((1, HG_DIM))],
        out_specs=col(0),
        out_shape=jax.ShapeDtypeStruct((t, HG_WIDTH), BF16),
        scratch_shapes=[pltpu.VMEM((HG_HEADS, HG_DIM, HG_DIM), F32)],
        compiler_params=pltpu.CompilerParams(
            dimension_semantics=("arbitrary", "arbitrary"), vmem_limit_bytes=VMEM_LIMIT),
        name="hgrn2",
    )(main, hf, main, main, lb, hg_g)


def _mla_up_body(mla_ref, qg_ref, kvg_ref, wqn_ref, wqa_ref, wqb_ref, wkn_ref, wv_ref,
                 cos_ref, sin_ref, q_ref, k_ref, v_ref, *, q_scale):
    lat = mla_ref[...].astype(F32)
    cq = _rms(lat[:, :Q_LORA], qg_ref[...]).astype(BF16)
    ckv = _rms(lat[:, Q_LORA:Q_LORA + KV_LORA], kvg_ref[...]).astype(BF16)
    cos = cos_ref[...]
    sin = sin_ref[...]
    kr_a = lat[:, Q_LORA + KV_LORA:Q_LORA + KV_LORA + LANES]
    kr_b = lat[:, Q_LORA + KV_LORA + LANES:]
    k_pe = (kr_a * cos + kr_b * sin).astype(BF16)

    q_nope = _dot(cq, wqn_ref[...])
    q_a = _dot(cq, wqa_ref[...])
    q_b = _dot(cq, wqb_ref[...])
    k_nope = _dot(ckv, wkn_ref[...])
    v = _dot(ckv, wv_ref[...])
    for h in range(MLA_HEADS):
        cs = slice(h * LANES, (h + 1) * LANES)
        q_ref[h, :, :LANES] = (q_nope[:, cs] * q_scale).astype(BF16)
        q_ref[h, :, LANES:] = ((q_a[:, cs] * cos + q_b[:, cs] * sin) * q_scale).astype(BF16)
        k_ref[h, :, :LANES] = k_nope[:, cs].astype(BF16)
        k_ref[h, :, LANES:] = k_pe
        v_ref[h] = v[:, cs].astype(BF16)


def _mla_up(mla, qg, kvg, wqn, wqa, wqb, wkn, wv, cos_t, sin_t, batch, seq, q_scale):
    nblk = seq // MLA_TS
    head_out = lambda w: pl.BlockSpec((MLA_HEADS, MLA_TS, w), lambda b, s: (b, s, 0))
    tab = pl.BlockSpec((MLA_TS, LANES), lambda b, s: (s, 0))
    return pl.pallas_call(
        functools.partial(_mla_up_body, q_scale=q_scale),
        grid=(batch, nblk),
        in_specs=[pl.BlockSpec((MLA_TS, MLA_W), lambda b, s: (b * nblk + s, 0)),
                  _resident(qg.shape), _resident(kvg.shape), _resident(wqn.shape),
                  _resident(wqa.shape), _resident(wqb.shape), _resident(wkn.shape),
                  _resident(wv.shape), tab, tab],
        out_specs=[head_out(QK_PAD), head_out(QK_PAD), head_out(V_DIM)],
        out_shape=[jax.ShapeDtypeStruct((batch * MLA_HEADS, seq, QK_PAD), BF16),
                   jax.ShapeDtypeStruct((batch * MLA_HEADS, seq, QK_PAD), BF16),
                   jax.ShapeDtypeStruct((batch * MLA_HEADS, seq, V_DIM), BF16)],
        compiler_params=pltpu.CompilerParams(
            dimension_semantics=("arbitrary", "arbitrary"), vmem_limit_bytes=VMEM_LIMIT),
        name="mla_up",
    )(mla, qg, kvg, wqn, wqa, wqb, wkn, wv, cos_t, sin_t)


def _attn_body(q_ref, k_ref, v_ref, mz_ref, o_ref, m_ref, l_ref, acc_ref):
    qi = pl.program_id(2)
    q = q_ref[0]
    m_ref[...] = jnp.full_like(m_ref, -jnp.inf)
    l_ref[...] = jnp.zeros_like(l_ref)
    acc_ref[...] = jnp.zeros_like(acc_ref)

    def block(j, masked):
        ks = pl.ds(pl.multiple_of(j * ATT_TK, ATT_TK), ATT_TK)
        s = _dot_nt(q, k_ref[0, ks, :])
        if masked:
            row = lax.broadcasted_iota(jnp.int32, s.shape, 0)
            col = lax.broadcasted_iota(jnp.int32, s.shape, 1)
            s = jnp.where(row >= col, s, -jnp.inf)
        m_prev = m_ref[...]
        m_new = jnp.maximum(m_prev, jnp.max(s, axis=-1, keepdims=True))
        alpha = jnp.exp2(m_prev - m_new)
        p = jnp.exp2(s - m_new)
        l_ref[...] = alpha * l_ref[...] + jnp.sum(p, axis=-1, keepdims=True)
        acc_ref[...] = alpha * acc_ref[...] + _dot(p.astype(BF16), v_ref[0, ks, :])
        m_ref[...] = m_new

    def off_diag(j, carry):
        block(j, masked=False)
        return carry

    lax.fori_loop(0, qi, off_diag, 0)
    block(qi, masked=True)

    mz = mz_ref[...].astype(F32)
    o_ref[...] = (acc_ref[...] / l_ref[...] * (mz * jax.nn.sigmoid(mz))).astype(BF16)


def _attn(q, k, v, main, batch, seq):
    nq = seq // ATT_TQ
    mz_col0 = 3 * D_MODEL // V_DIM
    kv_spec = lambda w: pl.BlockSpec((1, seq, w), lambda b, h, i: (b * MLA_HEADS + h, 0, 0))
    return pl.pallas_call(
        _attn_body,
        grid=(batch, MLA_HEADS, nq),
        in_specs=[pl.BlockSpec((1, ATT_TQ, QK_PAD), lambda b, h, i: (b * MLA_HEADS + h, i, 0)),
                  kv_spec(QK_PAD), kv_spec(V_DIM),
                  pl.BlockSpec((ATT_TQ, V_DIM), lambda b, h, i: (b * nq + i, mz_col0 + h))],
        out_specs=pl.BlockSpec((ATT_TQ, V_DIM), lambda b, h, i: (b * nq + i, h)),
        out_shape=jax.ShapeDtypeStruct((batch * seq, MLA_WIDTH), BF16),
        scratch_shapes=[pltpu.VMEM((ATT_TQ, 1), F32), pltpu.VMEM((ATT_TQ, 1), F32),
                        pltpu.VMEM((ATT_TQ, V_DIM), F32)],
        compiler_params=pltpu.CompilerParams(
            dimension_semantics=("arbitrary", "arbitrary", "arbitrary"),
            vmem_limit_bytes=VMEM_LIMIT),
        name="attn",
    )(q, k, v, main)


def _merge_body(x_ref, ya_ref, yb_ref, ga_ref, gb_ref, bg_ref, wpa_ref, wpb_ref, wout_ref,
                fg_ref, o_ref):
    gate_a = jax.nn.sigmoid(ga_ref[...].astype(F32) + bg_ref[:, :D_MODEL])
    gate_b = jax.nn.sigmoid(gb_ref[...].astype(F32) + bg_ref[:, D_MODEL:])
    merged = gate_a * _dot(ya_ref[...], wpa_ref[...]) + gate_b * _dot(yb_ref[...], wpb_ref[...])
    x_new = x_ref[...] + _dot(merged.astype(BF16), wout_ref[...])
    o_ref[...] = _rms(x_new, fg_ref[...])


def _merge(x2, ya, yb, main, b_gate, wpa, wpb, wout, fg):
    t = x2.shape[0]
    row = lambda j: pl.BlockSpec((MERGE_TM, D_MODEL), lambda i: (i, j))
    return pl.pallas_call(
        _merge_body,
        grid=(t // MERGE_TM,),
        in_specs=[row(0), row(0), row(0), row(4), row(5), _resident(b_gate.shape),
                  _resident(wpa.shape), _resident(wpb.shape), _resident(wout.shape),
                  _resident(fg.shape)],
        out_specs=row(0),
        out_shape=jax.ShapeDtypeStruct((t, D_MODEL), F32),
        compiler_params=pltpu.CompilerParams(
            dimension_semantics=("arbitrary",), vmem_limit_bytes=VMEM_LIMIT),
        name="merge",
    )(x2, ya, yb, main, main, b_gate, wpa, wpb, wout, fg)


def _rope_tables(seq):
    inv = ROPE_THETA ** (-jnp.arange(0, QK_ROPE, 2, dtype=F32) / QK_ROPE)
    ang = jnp.arange(seq, dtype=F32)[:, None] * inv[None, :]
    cos, sin = jnp.cos(ang), jnp.sin(ang)
    zero = jnp.zeros_like(cos)
    return (jnp.concatenate([cos, cos, zero, zero], axis=-1),
            jnp.concatenate([-sin, sin, zero, zero], axis=-1))


def _pad_heads(w, width):
    rows, heads, d = w.shape
    return jnp.pad(w, ((0, 0), (0, 0), (0, width - d))).reshape(rows, heads * width)


def _layer(x2, batch, seq, norm_g, w_in, b_gate, lb, hg_norm_g, q_a_g, w_uq, kv_a_g, w_ukv,
           w_proj_a, w_proj_b, w_out, out_g, cos_t, sin_t):
    half = QK_ROPE // 2
    o = 0
    w_hq, w_hf, w_hi, w_hz = (w_in[:, o + i * HG_WIDTH:o + (i + 1) * HG_WIDTH] for i in range(4))
    o += 4 * HG_WIDTH
    w_cq = w_in[:, o:o + Q_LORA]; o += Q_LORA
    w_ckv = w_in[:, o:o + KV_LORA]; o += KV_LORA
    w_kr = w_in[:, o:o + QK_ROPE]; o += QK_ROPE
    w_mz = w_in[:, o:o + MLA_WIDTH]; o += MLA_WIDTH
    w_gl = w_in[:, o:]
    kr_pad = jnp.zeros((D_MODEL, LANES - QK_ROPE), F32)
    w_kr_swapped = jnp.concatenate([w_kr[:, half:], w_kr[:, :half]], axis=1)
    w_main = jnp.concatenate([w_hq, w_hi, w_hz, w_mz, w_gl], axis=1).astype(BF16)
    w_mla = jnp.concatenate([w_cq, w_ckv, w_kr, kr_pad, w_kr_swapped, kr_pad], axis=1).astype(BF16)

    uq = w_uq.reshape(Q_LORA, MLA_HEADS, QK_DIM)
    wqn = uq[:, :, :QK_NOPE].reshape(Q_LORA, MLA_HEADS * QK_NOPE).astype(BF16)
    q1, q2 = uq[:, :, QK_NOPE:QK_NOPE + half], uq[:, :, QK_NOPE + half:]
    wqa = _pad_heads(jnp.concatenate([q1, q2], axis=-1), LANES).astype(BF16)
    wqb = _pad_heads(jnp.concatenate([q2, q1], axis=-1), LANES).astype(BF16)
    ukv = w_ukv.reshape(KV_LORA, MLA_HEADS, QK_NOPE + V_DIM)
    wkn = ukv[:, :, :QK_NOPE].reshape(KV_LORA, MLA_HEADS * QK_NOPE).astype(BF16)
    wv = ukv[:, :, QK_NOPE:].reshape(KV_LORA, MLA_WIDTH).astype(BF16)

    main, hf, mla = _proj(x2, norm_g[None], w_main, w_hf.astype(BF16), w_mla)
    y_a = _hgrn(main, hf, lb[None], hg_norm_g[None], batch, seq)
    q_scale = QK_DIM ** -0.5 * math.log2(math.e)
    q, k, v = _mla_up(mla, q_a_g[None], kv_a_g[None], wqn, wqa, wqb, wkn, wv, cos_t, sin_t,
                      batch, seq, q_scale)
    y_b = _attn(q, k, v, main, batch, seq)
    return _merge(x2, y_a, y_b, main, b_gate[None], w_proj_a.astype(BF16),
                  w_proj_b.astype(BF16), w_out.astype(BF16), out_g[None])


def kernel(x, norm_g, w_in, b_gate, lb_logits, hg_norm_g, q_a_g, w_uq, kv_a_g, w_ukv,
           w_proj_a, w_proj_b, w_out, final_norm_g):
    batch, seq, _ = x.shape
    depth = norm_g.shape[0]
    assert depth == 1, "the final RMSNorm is fused into the single layer's merge kernel"
    lower_bounds = jnp.cumsum(jax.nn.softmax(lb_logits.astype(F32), axis=0), axis=0)[:depth]
    cos_t, sin_t = _rope_tables(seq)
    x2 = x.reshape(batch * seq, D_MODEL)
    out = _layer(x2, batch, seq, norm_g[0], w_in[0], b_gate[0], lower_bounds[0], hg_norm_g[0],
                 q_a_g[0], w_uq[0], kv_a_g[0], w_ukv[0], w_proj_a[0], w_proj_b[0], w_out[0],
                 final_norm_g, cos_t, sin_t)
    return out.reshape(batch, seq, D_MODEL)
```

```python
import functools
import math

import jax
import jax.numpy as jnp
from jax import lax
from jax.experimental import pallas as pl
from jax.experimental.pallas import tpu as pltpu

F32 = jnp.float32
BF16 = jnp.bfloat16

D_MODEL = 1024
HG_HEADS = 8
HG_DIM = 128
HG_WIDTH = HG_HEADS * HG_DIM
HG_CHUNK = 32
MLA_HEADS = 8
QK_NOPE = 128
QK_ROPE = 64
QK_DIM = QK_NOPE + QK_ROPE
V_DIM = 128
Q_LORA = 3 * D_MODEL // 8
KV_LORA = D_MODEL // 4
MLA_WIDTH = MLA_HEADS * V_DIM
ROPE_THETA = 10000.0
EPS = 1e-6

LANES = 128
QK_PAD = 2 * LANES
HG_GROUP = 128
VMEM_LIMIT = 56 * 1024 * 1024

PROJ_TM = 512
PROJ_CW = 512
HG_TC = 512
MLA_TS = 512
ATT_TQ = 2048
ATT_TK = 512
ATT_LOOKAHEAD = 2
ATT_SLOTS = ATT_TQ // ATT_TK
ATT_UNROLL = 2
BF16_ROWS = 16
V_EXT = V_DIM + BF16_ROWS
MERGE_TM = 512

MAIN_W = 6 * D_MODEL
MLA_W = Q_LORA + KV_LORA + 2 * LANES


def _resident(shape):
    return pl.BlockSpec(shape, lambda *_: (0,) * len(shape), pipeline_mode=pl.Buffered(1))


def _rms(x, g):
    return x * lax.rsqrt(jnp.mean(x * x, axis=-1, keepdims=True) + EPS) * g


def _dot(a, b):
    return jnp.dot(a, b, preferred_element_type=F32)


def _dot_nt(a, b):
    return lax.dot_general(a, b, (((1,), (1,)), ((), ())), preferred_element_type=F32)


def _proj_body(x_ref, g_ref, wmain_ref, whf_ref, wmla_ref, main_ref, hf_ref, mla_ref):
    h = _rms(x_ref[...], g_ref[...]).astype(BF16)
    for c in range(MAIN_W // PROJ_CW):
        cs = slice(c * PROJ_CW, (c + 1) * PROJ_CW)
        main_ref[:, cs] = _dot(h, wmain_ref[:, cs]).astype(BF16)
    for c in range(HG_WIDTH // PROJ_CW):
        cs = slice(c * PROJ_CW, (c + 1) * PROJ_CW)
        hf_ref[:, cs] = _dot(h, whf_ref[:, cs])
    mla_ref[...] = _dot(h, wmla_ref[...]).astype(BF16)


def _proj(x2, norm_g, w_main, w_hf, w_mla):
    t = x2.shape[0]
    row = lambda w: pl.BlockSpec((PROJ_TM, w), lambda i: (i, 0))
    return pl.pallas_call(
        _proj_body,
        grid=(t // PROJ_TM,),
        in_specs=[row(D_MODEL), _resident((1, D_MODEL)), _resident(w_main.shape),
                  _resident(w_hf.shape), _resident(w_mla.shape)],
        out_specs=[row(MAIN_W), row(HG_WIDTH), row(MLA_W)],
        out_shape=[jax.ShapeDtypeStruct((t, MAIN_W), BF16),
                   jax.ShapeDtypeStruct((t, HG_WIDTH), F32),
                   jax.ShapeDtypeStruct((t, MLA_W), BF16)],
        compiler_params=pltpu.CompilerParams(
            dimension_semantics=("arbitrary",), vmem_limit_bytes=VMEM_LIMIT),
        name="proj",
    )(x2, norm_g, w_main, w_hf, w_mla)


def _hgrn_body(hq_ref, hf_ref, hi_ref, hz_ref, lb_ref, g_ref, o_ref, st_ref):
    @pl.when(pl.program_id(1) == 0)
    def _():
        st_ref[...] = jnp.zeros_like(st_ref)

    r = lax.broadcasted_iota(jnp.int32, (HG_GROUP, HG_GROUP), 0)
    c = lax.broadcasted_iota(jnp.int32, (HG_GROUP, HG_GROUP), 1)
    shift = HG_CHUNK.bit_length() - 1
    chunk_r = lax.shift_right_logical(r, shift)
    same_chunk = chunk_r == lax.shift_right_logical(c, shift)
    causal = same_chunk & (r >= c)
    sum_mat = jnp.concatenate([jnp.where(causal, 1.0, 0.0), jnp.where(same_chunk, 1.0, 0.0)],
                              axis=0).astype(BF16)
    n_chunks = HG_GROUP // HG_CHUNK

    def group(gi, carry):
        rows = pl.ds(pl.multiple_of(gi * HG_GROUP, HG_GROUP), HG_GROUP)
        for h in range(HG_HEADS):
            cols = slice(h * HG_DIM, (h + 1) * HG_DIM)
            lb = lb_ref[:, cols]
            f = lb + (1.0 - lb) * jax.nn.sigmoid(hf_ref[rows, cols])
            log_f = jnp.log(f)
            k = 1.0 - f
            lf_hi = log_f.astype(BF16)
            lf_lo = (log_f - lf_hi.astype(F32)).astype(BF16)
            sums = _dot(sum_mat, lf_hi) + _dot(sum_mat, lf_lo)
            b = sums[:HG_GROUP]
            b_last = sums[HG_GROUP:]
            hq = hq_ref[rows, cols].astype(F32)
            q_in = (hq * jax.nn.sigmoid(hq) * jnp.exp(b)).astype(BF16)
            k_in = (k * jnp.exp(-b)).astype(BF16)
            k_out = k * jnp.exp(b_last - b)
            decay = jnp.exp(b_last)
            v = hi_ref[rows, cols]
            v_t = v.astype(F32).T.astype(BF16)

            scores = jnp.where(causal, _dot_nt(q_in, k_in), 0.0).astype(BF16)
            o_intra = _dot(scores, v)

            st = st_ref[h]
            o_inter = []
            for ci in range(n_chunks):
                cr = slice(ci * HG_CHUNK, (ci + 1) * HG_CHUNK)
                o_inter.append(_dot_nt(q_in[cr], st.astype(BF16)))
                k_out_c = jnp.where(chunk_r == ci, k_out, 0.0).astype(BF16)
                st = st * decay[ci * HG_CHUNK:ci * HG_CHUNK + 1] + _dot(v_t, k_out_c)
            st_ref[h] = st
            o = o_intra + jnp.concatenate(o_inter, axis=0)

            hz = hz_ref[rows, cols].astype(F32)
            y = _rms(o, g_ref[...]) * (hz * jax.nn.sigmoid(hz))
            o_ref[rows, cols] = y.astype(BF16)
        return carry

    lax.fori_loop(0, HG_TC // HG_GROUP, group, 0)


def _hgrn(main, hf, lb, hg_g, batch, seq):
    nblk = seq // HG_TC
    t = batch * seq

    def col(j):
        return pl.BlockSpec((HG_TC, HG_WIDTH), lambda b, s: (b * nblk + s, j))

    return pl.pallas_call(
        _hgrn_body,
        grid=(batch, nblk),
        in_specs=[col(0), col(0), col(1), col(2),
                  _resident((1, HG_WIDTH)), _resident((1, HG_DIM))],
        out_specs=col(0),
        out_shape=jax.ShapeDtypeStruct((t, HG_WIDTH), BF16),
        scratch_shapes=[pltpu.VMEM((HG_HEADS, HG_DIM, HG_DIM), F32)],
        compiler_params=pltpu.CompilerParams(
            dimension_semantics=("arbitrary", "arbitrary"), vmem_limit_bytes=VMEM_LIMIT),
        name="hgrn2",
    )(main, hf, main, main, lb, hg_g)


def _mla_up_body(mla_ref, qg_ref, kvg_ref, wqa_ref, wqb_ref, wkn_ref, wv_ref,
                 cos_ref, sin_ref, cos_t_ref, sin_t_ref, qt_ref, k_ref, vt_ref, *, q_scale):
    lat = mla_ref[...].astype(F32)
    cq = _rms(lat[:, :Q_LORA], qg_ref[...]).astype(BF16)
    ckv = _rms(lat[:, Q_LORA:Q_LORA + KV_LORA], kvg_ref[...]).astype(BF16)
    kr_a = lat[:, Q_LORA + KV_LORA:Q_LORA + KV_LORA + LANES]
    kr_b = lat[:, Q_LORA + KV_LORA + LANES:]
    k_pe = (kr_a * cos_ref[...] + kr_b * sin_ref[...]).astype(BF16)
    k_nope = _dot(ckv, wkn_ref[...])
    qa_t = _dot_nt(wqa_ref[...], cq)
    qb_t = _dot_nt(wqb_ref[...], cq)
    v_t = _dot_nt(wv_ref[...], ckv)
    cos_t = cos_t_ref[...]
    sin_t = sin_t_ref[...]
    ones_row = jnp.where(lax.broadcasted_iota(jnp.int32, (BF16_ROWS, ATT_TK), 0) == 0,
                         1.0, 0.0).astype(BF16)
    for h in range(MLA_HEADS):
        hs = slice(h * LANES, (h + 1) * LANES)
        qt_ref[h, :LANES, :] = (qa_t[h * QK_PAD:h * QK_PAD + LANES] * q_scale).astype(BF16)
        q_pe = qa_t[h * QK_PAD + LANES:(h + 1) * QK_PAD] * cos_t + qb_t[hs] * sin_t
        qt_ref[h, LANES:, :] = (q_pe * q_scale).astype(BF16)
        k_ref[h, :, :LANES] = k_nope[:, hs].astype(BF16)
        k_ref[h, :, LANES:] = k_pe
        for c in range(MLA_TS // ATT_TK):
            vt_ref[h, c, :V_DIM, :] = v_t[hs, c * ATT_TK:(c + 1) * ATT_TK].astype(BF16)
            vt_ref[h, c, V_DIM:, :] = ones_row


def _mla_up(mla, qg, kvg, wqa, wqb, wkn, wv, tabs, batch, seq, q_scale):
    nblk = seq // MLA_TS
    kv_per_step = MLA_TS // ATT_TK
    tab = pl.BlockSpec((MLA_TS, LANES), lambda b, s: (s, 0))
    tab_t = pl.BlockSpec((LANES, MLA_TS), lambda b, s: (0, s))
    return pl.pallas_call(
        functools.partial(_mla_up_body, q_scale=q_scale),
        grid=(batch, nblk),
        in_specs=[pl.BlockSpec((MLA_TS, MLA_W), lambda b, s: (b * nblk + s, 0)),
                  _resident(qg.shape), _resident(kvg.shape), _resident(wqa.shape),
                  _resident(wqb.shape), _resident(wkn.shape), _resident(wv.shape),
                  tab, tab, tab_t, tab_t],
        out_specs=[pl.BlockSpec((MLA_HEADS, QK_PAD, MLA_TS), lambda b, s: (b, 0, s)),
                   pl.BlockSpec((MLA_HEADS, MLA_TS, QK_PAD), lambda b, s: (b, s, 0)),
                   pl.BlockSpec((MLA_HEADS, kv_per_step, V_EXT, ATT_TK),
                                lambda b, s: (b, s, 0, 0))],
        out_shape=[jax.ShapeDtypeStruct((batch * MLA_HEADS, QK_PAD, seq), BF16),
                   jax.ShapeDtypeStruct((batch * MLA_HEADS, seq, QK_PAD), BF16),
                   jax.ShapeDtypeStruct((batch * MLA_HEADS, seq // ATT_TK, V_EXT, ATT_TK), BF16)],
        compiler_params=pltpu.CompilerParams(
            dimension_semantics=("arbitrary", "arbitrary"), vmem_limit_bytes=VMEM_LIMIT),
        name="mla_up",
    )(mla, qg, kvg, wqa, wqb, wkn, wv, *tabs)


def _attn_body(qt_ref, k_ref, vt_ref, mz_ref, o_ref, s_ref, acc_ref, m_ref):
    qi = pl.program_id(2)
    n_strips = ATT_TQ // ATT_TK

    def lanes_of(si):
        return slice(si * ATT_TK, (si + 1) * ATT_TK)

    def scores(item):
        j, si, _ = item
        ks = pl.ds(pl.multiple_of(j * ATT_TK, ATT_TK), ATT_TK)
        return _dot(k_ref[0, ks, :], qt_ref[0, :, lanes_of(si)])

    def softmax_pv(item, s):
        j, si, diagonal = item
        lanes = lanes_of(si)
        if diagonal:
            key = lax.broadcasted_iota(jnp.int32, s.shape, 0)
            query = lax.broadcasted_iota(jnp.int32, s.shape, 1)
            s = jnp.where(query >= key, s, -jnp.inf)
        m_prev = m_ref[:, lanes]
        m_new = jnp.maximum(m_prev, jnp.max(s, axis=0, keepdims=True))
        alpha = jnp.exp2(m_prev - m_new)
        p = jnp.exp2(s - m_new).astype(BF16)
        acc_ref[:, lanes] = alpha * acc_ref[:, lanes] + _dot(vt_ref[0, j], p)
        m_ref[:, lanes] = m_new

    def run(items, n_consume):
        for idx in range(n_consume):
            ahead = idx + ATT_LOOKAHEAD
            if ahead < len(items):
                s_ref[ahead % ATT_SLOTS] = scores(items[ahead])
            softmax_pv(items[idx], s_ref[idx % ATT_SLOTS])

    def key_blocks(t, carry):
        j = t * ATT_UNROLL
        items = [(j + b, si, False) for b in range(ATT_UNROLL + 1) for si in range(n_strips)]
        run(items, ATT_UNROLL * n_strips)
        return carry

    first = qi * n_strips
    diagonal = [(first + c, si, si == c) for c in range(n_strips) for si in range(c, n_strips)]

    acc_ref[...] = jnp.zeros_like(acc_ref)
    m_ref[...] = jnp.full_like(m_ref, -jnp.inf)
    for idx in range(ATT_LOOKAHEAD):
        s_ref[idx] = scores((0, idx, False))
    lax.fori_loop(0, qi * (n_strips // ATT_UNROLL), key_blocks, 0)
    run(diagonal, len(diagonal))

    mz = mz_ref[...].astype(F32)
    o_t = acc_ref[:V_DIM] * (1.0 / acc_ref[V_DIM:V_DIM + 1])
    o_ref[...] = (o_t.T * (mz * jax.nn.sigmoid(mz))).astype(BF16)


def _attn(qt, k, vt, main, batch, seq):
    assert ATT_TQ % ATT_TK == 0
    nq = seq // ATT_TQ
    mz_col0 = 3 * D_MODEL // V_DIM
    head = lambda b, h, i: b * MLA_HEADS + h
    return pl.pallas_call(
        _attn_body,
        grid=(batch, MLA_HEADS, nq),
        in_specs=[pl.BlockSpec((1, QK_PAD, ATT_TQ), lambda b, h, i: (head(b, h, i), 0, i)),
                  pl.BlockSpec((1, seq, QK_PAD), lambda b, h, i: (head(b, h, i), 0, 0)),
                  pl.BlockSpec((1, seq // ATT_TK, V_EXT, ATT_TK),
                               lambda b, h, i: (head(b, h, i), 0, 0, 0)),
                  pl.BlockSpec((ATT_TQ, V_DIM), lambda b, h, i: (b * nq + i, mz_col0 + h))],
        out_specs=pl.BlockSpec((ATT_TQ, V_DIM), lambda b, h, i: (b * nq + i, h)),
        out_shape=jax.ShapeDtypeStruct((batch * seq, MLA_WIDTH), BF16),
        scratch_shapes=[pltpu.VMEM((ATT_SLOTS, ATT_TK, ATT_TK), F32),
                        pltpu.VMEM((V_EXT, ATT_TQ), F32), pltpu.VMEM((1, ATT_TQ), F32)],
        compiler_params=pltpu.CompilerParams(
            dimension_semantics=("arbitrary", "arbitrary", "arbitrary"),
            vmem_limit_bytes=VMEM_LIMIT),
        name="attn",
    )(qt, k, vt, main)


def _merge_body(x_ref, ya_ref, yb_ref, ga_ref, gb_ref, bg_ref, wpa_ref, wpb_ref, wout_ref,
                fg_ref, o_ref):
    gate_a = jax.nn.sigmoid(ga_ref[...].astype(F32) + bg_ref[:, :D_MODEL])
    gate_b = jax.nn.sigmoid(gb_ref[...].astype(F32) + bg_ref[:, D_MODEL:])
    merged = gate_a * _dot(ya_ref[...], wpa_ref[...]) + gate_b * _dot(yb_ref[...], wpb_ref[...])
    x_new = x_ref[...] + _dot(merged.astype(BF16), wout_ref[...])
    o_ref[...] = _rms(x_new, fg_ref[...])


def _merge(x2, ya, yb, main, b_gate, wpa, wpb, wout, fg):
    t = x2.shape[0]
    row = lambda j: pl.BlockSpec((MERGE_TM, D_MODEL), lambda i: (i, j))
    return pl.pallas_call(
        _merge_body,
        grid=(t // MERGE_TM,),
        in_specs=[row(0), row(0), row(0), row(4), row(5), _resident(b_gate.shape),
                  _resident(wpa.shape), _resident(wpb.shape), _resident(wout.shape),
                  _resident(fg.shape)],
        out_specs=row(0),
        out_shape=jax.ShapeDtypeStruct((t, D_MODEL), F32),
        compiler_params=pltpu.CompilerParams(
            dimension_semantics=("arbitrary",), vmem_limit_bytes=VMEM_LIMIT),
        name="merge",
    )(x2, ya, yb, main, main, b_gate, wpa, wpb, wout, fg)


def _rope_tables(seq):
    inv = ROPE_THETA ** (-jnp.arange(0, QK_ROPE, 2, dtype=F32) / QK_ROPE)
    ang = jnp.arange(seq, dtype=F32)[:, None] * inv[None, :]
    cos, sin = jnp.cos(ang), jnp.sin(ang)
    zero = jnp.zeros_like(cos)
    cos_tab = jnp.concatenate([cos, cos, zero, zero], axis=-1)
    sin_tab = jnp.concatenate([-sin, sin, zero, zero], axis=-1)
    return cos_tab, sin_tab, cos_tab.T, sin_tab.T


def _pad_heads(w, width):
    rows, heads, d = w.shape
    return jnp.pad(w, ((0, 0), (0, 0), (0, width - d))).reshape(rows, heads * width)


def _layer(x2, batch, seq, norm_g, w_in, b_gate, lb, hg_norm_g, q_a_g, w_uq, kv_a_g, w_ukv,
           w_proj_a, w_proj_b, w_out, out_g, rope_tabs):
    half = QK_ROPE // 2
    o = 0
    w_hq, w_hf, w_hi, w_hz = (w_in[:, o + i * HG_WIDTH:o + (i + 1) * HG_WIDTH] for i in range(4))
    o += 4 * HG_WIDTH
    w_cq = w_in[:, o:o + Q_LORA]; o += Q_LORA
    w_ckv = w_in[:, o:o + KV_LORA]; o += KV_LORA
    w_kr = w_in[:, o:o + QK_ROPE]; o += QK_ROPE
    w_mz = w_in[:, o:o + MLA_WIDTH]; o += MLA_WIDTH
    w_gl = w_in[:, o:]
    kr_pad = jnp.zeros((D_MODEL, LANES - QK_ROPE), F32)
    w_kr_swapped = jnp.concatenate([w_kr[:, half:], w_kr[:, :half]], axis=1)
    w_main = jnp.concatenate([w_hq, w_hi, w_hz, w_mz, w_gl], axis=1).astype(BF16)
    w_mla = jnp.concatenate([w_cq, w_ckv, w_kr, kr_pad, w_kr_swapped, kr_pad], axis=1).astype(BF16)

    uq = w_uq.reshape(Q_LORA, MLA_HEADS, QK_DIM)
    q1, q2 = uq[:, :, QK_NOPE:QK_NOPE + half], uq[:, :, QK_NOPE + half:]
    wqa = _pad_heads(jnp.concatenate([uq[:, :, :QK_NOPE], q1, q2], axis=-1), QK_PAD).T.astype(BF16)
    wqb = _pad_heads(jnp.concatenate([q2, q1], axis=-1), LANES).T.astype(BF16)
    ukv = w_ukv.reshape(KV_LORA, MLA_HEADS, QK_NOPE + V_DIM)
    wkn = ukv[:, :, :QK_NOPE].reshape(KV_LORA, MLA_HEADS * QK_NOPE).astype(BF16)
    wv = ukv[:, :, QK_NOPE:].reshape(KV_LORA, MLA_WIDTH).T.astype(BF16)

    main, hf, mla = _proj(x2, norm_g[None], w_main, w_hf.astype(BF16), w_mla)
    y_a = _hgrn(main, hf, lb[None], hg_norm_g[None], batch, seq)
    q_scale = QK_DIM ** -0.5 * math.log2(math.e)
    qt, k, vt = _mla_up(mla, q_a_g[None], kv_a_g[None], wqa, wqb, wkn, wv, rope_tabs,
                        batch, seq, q_scale)
    y_b = _attn(qt, k, vt, main, batch, seq)
    return _merge(x2, y_a, y_b, main, b_gate[None], w_proj_a.astype(BF16),
                  w_proj_b.astype(BF16), w_out.astype(BF16), out_g[None])


def kernel(x, norm_g, w_in, b_gate, lb_logits, hg_norm_g, q_a_g, w_uq, kv_a_g, w_ukv,
           w_proj_a, w_proj_b, w_out, final_norm_g):
    batch, seq, _ = x.shape
    depth = norm_g.shape[0]
    assert depth == 1, "the final RMSNorm is fused into the single layer's merge kernel"
    lower_bounds = jnp.cumsum(jax.nn.softmax(lb_logits.astype(F32), axis=0), axis=0)[:depth]
    rope_tabs = _rope_tables(seq)
    x2 = x.reshape(batch * seq, D_MODEL)
    out = _layer(x2, batch, seq, norm_g[0], w_in[0], b_gate[0], lower_bounds[0], hg_norm_g[0],
                 q_a_g[0], w_uq[0], kv_a_g[0], w_ukv[0], w_proj_a[0], w_proj_b[0], w_out[0],
                 final_norm_g, rope_tabs)
    return out.reshape(batch, seq, D_MODEL)
```

```python
import functools
import math

import jax
import jax.numpy as jnp
from jax import lax
from jax.experimental import pallas as pl
from jax.experimental.pallas import tpu as pltpu

F32 = jnp.float32
BF16 = jnp.bfloat16

D_MODEL = 1024
HG_HEADS = 8
HG_DIM = 128
HG_WIDTH = HG_HEADS * HG_DIM
HG_CHUNK = 32
MLA_HEADS = 8
QK_NOPE = 128
QK_ROPE = 64
QK_DIM = QK_NOPE + QK_ROPE
V_DIM = 128
Q_LORA = 3 * D_MODEL // 8
KV_LORA = D_MODEL // 4
MLA_WIDTH = MLA_HEADS * V_DIM
ROPE_THETA = 10000.0
EPS = 1e-6

LANES = 128
QK_PAD = 2 * LANES
HG_GROUP = 128
VMEM_LIMIT = 56 * 1024 * 1024

PROJ_TM = 512
PROJ_CW = 512
HG_TC = 512
MLA_TS = 512
ATT_TQ = 2048
ATT_TK = 512
ATT_LOOKAHEAD = 2
ATT_SLOTS = ATT_TQ // ATT_TK
ATT_UNROLL = 2
BF16_ROWS = 16
V_EXT = V_DIM + BF16_ROWS
MERGE_TM = 512

MAIN_SECTIONS = ("q", "v", "z_a", "z_b", "gates", "gates")
MAIN_W = len(MAIN_SECTIONS) * D_MODEL
MLA_W = Q_LORA + KV_LORA + 2 * LANES


def _resident(shape):
    return pl.BlockSpec(shape, lambda *_: (0,) * len(shape), pipeline_mode=pl.Buffered(1))


def _rms(x, g):
    return x * lax.rsqrt(jnp.mean(x * x, axis=-1, keepdims=True) + EPS) * g


def _dot(a, b):
    return jnp.dot(a, b, preferred_element_type=F32)


def _dot_nt(a, b):
    return lax.dot_general(a, b, (((1,), (1,)), ((), ())), preferred_element_type=F32)


def _silu(x):
    return x * jax.nn.sigmoid(x)


def _proj_body(x_ref, g_ref, lb_ref, bg_ref, wmain_ref, whf_ref, wmla_ref,
               main_ref, lfh_ref, lfl_ref, k_ref, mla_ref):
    h = _rms(x_ref[...], g_ref[...]).astype(BF16)
    for c in range(MAIN_W // PROJ_CW):
        cs = slice(c * PROJ_CW, (c + 1) * PROJ_CW)
        section = MAIN_SECTIONS[c * PROJ_CW // D_MODEL]
        y = _dot(h, wmain_ref[:, cs])
        if section in ("q", "z_a", "z_b"):
            y = _silu(y)
        elif section == "gates":
            gate_col = c * PROJ_CW - MAIN_SECTIONS.index("gates") * D_MODEL
            y = jax.nn.sigmoid(y + bg_ref[:, gate_col:gate_col + PROJ_CW])
        main_ref[:, cs] = y.astype(BF16)
    for c in range(HG_WIDTH // PROJ_CW):
        cs = slice(c * PROJ_CW, (c + 1) * PROJ_CW)
        lb = lb_ref[:, cs]
        f = lb + (1.0 - lb) * jax.nn.sigmoid(_dot(h, whf_ref[:, cs]))
        log_f = jnp.log(f)
        lf_hi = log_f.astype(BF16)
        lfh_ref[:, cs] = lf_hi
        lfl_ref[:, cs] = (log_f - lf_hi.astype(F32)).astype(BF16)
        k_ref[:, cs] = (1.0 - f).astype(BF16)
    mla_ref[...] = _dot(h, wmla_ref[...]).astype(BF16)


def _proj(x2, norm_g, lb, b_gate, w_main, w_hf, w_mla):
    t = x2.shape[0]
    row = lambda w: pl.BlockSpec((PROJ_TM, w), lambda i: (i, 0))
    out = lambda w: jax.ShapeDtypeStruct((t, w), BF16)
    return pl.pallas_call(
        _proj_body,
        grid=(t // PROJ_TM,),
        in_specs=[row(D_MODEL), _resident((1, D_MODEL)), _resident(lb.shape),
                  _resident(b_gate.shape), _resident(w_main.shape),
                  _resident(w_hf.shape), _resident(w_mla.shape)],
        out_specs=[row(MAIN_W), row(HG_WIDTH), row(HG_WIDTH), row(HG_WIDTH), row(MLA_W)],
        out_shape=[out(MAIN_W), out(HG_WIDTH), out(HG_WIDTH), out(HG_WIDTH), out(MLA_W)],
        compiler_params=pltpu.CompilerParams(
            dimension_semantics=("arbitrary",), vmem_limit_bytes=VMEM_LIMIT),
        name="proj",
    )(x2, norm_g, lb, b_gate, w_main, w_hf, w_mla)


def _hgrn_body(q_ref, lfh_ref, lfl_ref, k_ref, v_ref, z_ref, g_ref, o_ref, st_ref):
    @pl.when(pl.program_id(1) == 0)
    def _():
        st_ref[...] = jnp.zeros_like(st_ref)

    n_chunks = HG_GROUP // HG_CHUNK
    shift = HG_CHUNK.bit_length() - 1
    r = lax.broadcasted_iota(jnp.int32, (HG_GROUP, HG_GROUP), 0)
    c = lax.broadcasted_iota(jnp.int32, (HG_GROUP, HG_GROUP), 1)
    col_chunk = lax.shift_right_logical(c, shift)
    causal = (lax.shift_right_logical(r, shift) == col_chunk) & (r >= c)
    prefix_mat = jnp.where(causal, 1.0, 0.0).astype(BF16)
    r_blk = lax.broadcasted_iota(jnp.int32, (HG_CHUNK, HG_GROUP), 0)
    c_blk = lax.broadcasted_iota(jnp.int32, (HG_CHUNK, HG_GROUP), 1)
    key_chunk_is = [lax.shift_right_logical(c_blk, shift) == kc for kc in range(n_chunks)]
    causal_blk = [key_chunk_is[qc] & (c_blk <= r_blk + qc * HG_CHUNK) for qc in range(n_chunks)]
    pairs = [(kc + gap, kc) for gap in range(1, n_chunks) for kc in range(n_chunks - gap)]

    def chunk_rows(ci):
        return slice(ci * HG_CHUNK, (ci + 1) * HG_CHUNK)

    def by_chunk(fn, x):
        return jnp.concatenate([fn(ci, x[chunk_rows(ci)]) for ci in range(n_chunks)], axis=0)

    def group(gi, carry):
        rows = pl.ds(pl.multiple_of(gi * HG_GROUP, HG_GROUP), HG_GROUP)
        b_all = (_dot(prefix_mat, lfh_ref[rows, :])
                 + _dot(prefix_mat, lfl_ref[rows, :]))

        partial = []
        for h in range(HG_HEADS):
            cols = slice(h * HG_DIM, (h + 1) * HG_DIM)
            b = b_all[:, cols]
            k = k_ref[rows, cols].astype(F32)
            tot = [b[(ci + 1) * HG_CHUNK - 1:(ci + 1) * HG_CHUNK] for ci in range(n_chunks)]
            zero = jnp.zeros_like(tot[0])
            before = [zero]
            for ci in range(1, n_chunks):
                before.append(before[-1] + tot[ci - 1])
            after = [zero]
            for ci in range(n_chunks - 2, -1, -1):
                after.insert(0, after[0] + tot[ci + 1])
            q_in = q_ref[rows, cols].astype(F32) * jnp.exp(b)
            k_in = (k * jnp.exp(-b)).astype(BF16)
            k_out = k * jnp.exp(by_chunk(lambda ci, x: tot[ci] - x, b))
            q_start = by_chunk(lambda ci, x: x * jnp.exp(before[ci]), q_in).astype(BF16)
            k_end = by_chunk(lambda ci, x: x * jnp.exp(after[ci]), k_out).astype(BF16)
            q_cross = []
            for qc, kc in pairs:
                q_blk = q_in[chunk_rows(qc)]
                if qc > kc + 1:
                    q_blk = q_blk * jnp.exp(before[qc] - before[kc + 1])
                q_cross.append(q_blk)
            q_cross = jnp.concatenate(q_cross, axis=0).astype(BF16)

            v = v_ref[rows, cols]
            v_t = v.astype(F32).T.astype(BF16)
            st = st_ref[h]
            same = _dot_nt(q_in.astype(BF16), k_in)
            cross = _dot_nt(q_cross, k_out.astype(BF16))
            o_start = _dot_nt(q_start, st.astype(BF16))
            st_ref[h] = st * jnp.exp(before[-1] + tot[-1]) + _dot(v_t, k_end)
            partial.append((same, cross, o_start, v))

        for h in range(HG_HEADS):
            cols = slice(h * HG_DIM, (h + 1) * HG_DIM)
            same, cross, o_start, v = partial[h]
            score_rows = []
            for qc in range(n_chunks):
                blk = jnp.where(causal_blk[qc], same[chunk_rows(qc)], 0.0)
                for idx, (pq, pk) in enumerate(pairs):
                    if pq == qc:
                        blk = jnp.where(key_chunk_is[pk], cross[chunk_rows(idx)], blk)
                score_rows.append(blk)
            scores = jnp.concatenate(score_rows, axis=0).astype(BF16)
            o = _dot(scores, v) + o_start
            y = _rms(o, g_ref[...]) * z_ref[rows, cols].astype(F32)
            o_ref[rows, cols] = y.astype(BF16)
        return carry

    lax.fori_loop(0, HG_TC // HG_GROUP, group, 0)


def _hgrn(main, lf_hi, lf_lo, k, hg_g, batch, seq):
    nblk = seq // HG_TC
    t = batch * seq

    def col(name=None):
        j = 0 if name is None else MAIN_SECTIONS.index(name)
        return pl.BlockSpec((HG_TC, HG_WIDTH), lambda b, s: (b * nblk + s, j))

    return pl.pallas_call(
        _hgrn_body,
        grid=(batch, nblk),
        in_specs=[col("q"), col(), col(), col(), col("v"), col("z_a"), _resident((1, HG_DIM))],
        out_specs=col(),
        out_shape=jax.ShapeDtypeStruct((t, HG_WIDTH), BF16),
        scratch_shapes=[pltpu.VMEM((HG_HEADS, HG_DIM, HG_DIM), F32)],
        compiler_params=pltpu.CompilerParams(
            dimension_semantics=("arbitrary", "arbitrary"), vmem_limit_bytes=VMEM_LIMIT),
        name="hgrn2",
    )(main, lf_hi, lf_lo, k, main, main, hg_g)


def _mla_up_body(mla_ref, qg_ref, kvg_ref, wqa_ref, wqb_ref, wkn_ref, wv_ref,
                 cos_ref, sin_ref, cos_t_ref, sin_t_ref, qt_ref, k_ref, vt_ref, *, q_scale):
    lat = mla_ref[...].astype(F32)
    cq = _rms(lat[:, :Q_LORA], qg_ref[...]).astype(BF16)
    ckv = _rms(lat[:, Q_LORA:Q_LORA + KV_LORA], kvg_ref[...]).astype(BF16)
    kr_a = lat[:, Q_LORA + KV_LORA:Q_LORA + KV_LORA + LANES]
    kr_b = lat[:, Q_LORA + KV_LORA + LANES:]
    k_pe = (kr_a * cos_ref[...] + kr_b * sin_ref[...]).astype(BF16)
    k_nope = _dot(ckv, wkn_ref[...])
    qa_t = _dot_nt(wqa_ref[...], cq)
    qb_t = _dot_nt(wqb_ref[...], cq)
    v_t = _dot_nt(wv_ref[...], ckv)
    cos_t = cos_t_ref[...]
    sin_t = sin_t_ref[...]
    ones_row = jnp.where(lax.broadcasted_iota(jnp.int32, (BF16_ROWS, ATT_TK), 0) == 0,
                         1.0, 0.0).astype(BF16)
    for h in range(MLA_HEADS):
        hs = slice(h * LANES, (h + 1) * LANES)
        qt_ref[h, :LANES, :] = (qa_t[h * QK_PAD:h * QK_PAD + LANES] * q_scale).astype(BF16)
        q_pe = qa_t[h * QK_PAD + LANES:(h + 1) * QK_PAD] * cos_t + qb_t[hs] * sin_t
        qt_ref[h, LANES:, :] = (q_pe * q_scale).astype(BF16)
        k_ref[h, :, :LANES] = k_nope[:, hs].astype(BF16)
        k_ref[h, :, LANES:] = k_pe
        for c in range(MLA_TS // ATT_TK):
            vt_ref[h, c, :V_DIM, :] = v_t[hs, c * ATT_TK:(c + 1) * ATT_TK].astype(BF16)
            vt_ref[h, c, V_DIM:, :] = ones_row


def _mla_up(mla, qg, kvg, wqa, wqb, wkn, wv, tabs, batch, seq, q_scale):
    nblk = seq // MLA_TS
    kv_per_step = MLA_TS // ATT_TK
    tab = pl.BlockSpec((MLA_TS, LANES), lambda b, s: (s, 0))
    tab_t = pl.BlockSpec((LANES, MLA_TS), lambda b, s: (0, s))
    return pl.pallas_call(
        functools.partial(_mla_up_body, q_scale=q_scale),
        grid=(batch, nblk),
        in_specs=[pl.BlockSpec((MLA_TS, MLA_W), lambda b, s: (b * nblk + s, 0)),
                  _resident(qg.shape), _resident(kvg.shape), _resident(wqa.shape),
                  _resident(wqb.shape), _resident(wkn.shape), _resident(wv.shape),
                  tab, tab, tab_t, tab_t],
        out_specs=[pl.BlockSpec((MLA_HEADS, QK_PAD, MLA_TS), lambda b, s: (b, 0, s)),
                   pl.BlockSpec((MLA_HEADS, MLA_TS, QK_PAD), lambda b, s: (b, s, 0)),
                   pl.BlockSpec((MLA_HEADS, kv_per_step, V_EXT, ATT_TK),
                                lambda b, s: (b, s, 0, 0))],
        out_shape=[jax.ShapeDtypeStruct((batch * MLA_HEADS, QK_PAD, seq), BF16),
                   jax.ShapeDtypeStruct((batch * MLA_HEADS, seq, QK_PAD), BF16),
                   jax.ShapeDtypeStruct((batch * MLA_HEADS, seq // ATT_TK, V_EXT, ATT_TK), BF16)],
        compiler_params=pltpu.CompilerParams(
            dimension_semantics=("arbitrary", "arbitrary"), vmem_limit_bytes=VMEM_LIMIT),
        name="mla_up",
    )(mla, qg, kvg, wqa, wqb, wkn, wv, *tabs)


def _attn_body(qt_ref, k_ref, vt_ref, z_ref, o_ref, s_ref, acc_ref, m_ref):
    qi = pl.program_id(2)
    n_strips = ATT_TQ // ATT_TK

    def lanes_of(si):
        return slice(si * ATT_TK, (si + 1) * ATT_TK)

    def scores(item):
        j, si, _ = item
        ks = pl.ds(pl.multiple_of(j * ATT_TK, ATT_TK), ATT_TK)
        return _dot(k_ref[0, ks, :], qt_ref[0, :, lanes_of(si)])

    def softmax_pv(item, s):
        j, si, diagonal = item
        lanes = lanes_of(si)
        if diagonal:
            key = lax.broadcasted_iota(jnp.int32, s.shape, 0)
            query = lax.broadcasted_iota(jnp.int32, s.shape, 1)
            s = jnp.where(query >= key, s, -jnp.inf)
        m_prev = m_ref[:, lanes]
        m_new = jnp.maximum(m_prev, jnp.max(s, axis=0, keepdims=True))
        alpha = jnp.exp2(m_prev - m_new)
        p = jnp.exp2(s - m_new).astype(BF16)
        acc_ref[:, lanes] = alpha * acc_ref[:, lanes] + _dot(vt_ref[0, j], p)
        m_ref[:, lanes] = m_new

    def run(items, n_consume):
        for idx in range(n_consume):
            ahead = idx + ATT_LOOKAHEAD
            if ahead < len(items):
                s_ref[ahead % ATT_SLOTS] = scores(items[ahead])
            softmax_pv(items[idx], s_ref[idx % ATT_SLOTS])

    def key_blocks(t, carry):
        j = t * ATT_UNROLL
        items = [(j + b, si, False) for b in range(ATT_UNROLL + 1) for si in range(n_strips)]
        run(items, ATT_UNROLL * n_strips)
        return carry

    first = qi * n_strips
    diagonal = [(first + c, si, si == c) for c in range(n_strips) for si in range(c, n_strips)]

    acc_ref[...] = jnp.zeros_like(acc_ref)
    m_ref[...] = jnp.full_like(m_ref, -jnp.inf)
    for idx in range(ATT_LOOKAHEAD):
        s_ref[idx] = scores((0, idx, False))
    lax.fori_loop(0, qi * (n_strips // ATT_UNROLL), key_blocks, 0)
    run(diagonal, len(diagonal))

    o_t = acc_ref[:V_DIM] * (1.0 / acc_ref[V_DIM:V_DIM + 1])
    o_ref[...] = (o_t.T * z_ref[...].astype(F32)).astype(BF16)


def _attn(qt, k, vt, main, batch, seq):
    assert ATT_TQ % ATT_TK == 0
    nq = seq // ATT_TQ
    z_col0 = MAIN_SECTIONS.index("z_b") * D_MODEL // V_DIM
    head = lambda b, h, i: b * MLA_HEADS + h
    return pl.pallas_call(
        _attn_body,
        grid=(batch, MLA_HEADS, nq),
        in_specs=[pl.BlockSpec((1, QK_PAD, ATT_TQ), lambda b, h, i: (head(b, h, i), 0, i)),
                  pl.BlockSpec((1, seq, QK_PAD), lambda b, h, i: (head(b, h, i), 0, 0)),
                  pl.BlockSpec((1, seq // ATT_TK, V_EXT, ATT_TK),
                               lambda b, h, i: (head(b, h, i), 0, 0, 0)),
                  pl.BlockSpec((ATT_TQ, V_DIM), lambda b, h, i: (b * nq + i, z_col0 + h))],
        out_specs=pl.BlockSpec((ATT_TQ, V_DIM), lambda b, h, i: (b * nq + i, h)),
        out_shape=jax.ShapeDtypeStruct((batch * seq, MLA_WIDTH), BF16),
        scratch_shapes=[pltpu.VMEM((ATT_SLOTS, ATT_TK, ATT_TK), F32),
                        pltpu.VMEM((V_EXT, ATT_TQ), F32), pltpu.VMEM((1, ATT_TQ), F32)],
        compiler_params=pltpu.CompilerParams(
            dimension_semantics=("arbitrary", "arbitrary", "arbitrary"),
            vmem_limit_bytes=VMEM_LIMIT),
        name="attn",
    )(qt, k, vt, main)


def _merge_body(x_ref, ya_ref, yb_ref, ga_ref, gb_ref, wpa_ref, wpb_ref, wout_ref, fg_ref, o_ref):
    merged = (ga_ref[...].astype(F32) * _dot(ya_ref[...], wpa_ref[...])
              + gb_ref[...].astype(F32) * _dot(yb_ref[...], wpb_ref[...]))
    x_new = x_ref[...] + _dot(merged.astype(BF16), wout_ref[...])
    o_ref[...] = _rms(x_new, fg_ref[...])


def _merge(x2, ya, yb, main, wpa, wpb, wout, fg):
    t = x2.shape[0]
    gates = MAIN_SECTIONS.index("gates")
    row = lambda j: pl.BlockSpec((MERGE_TM, D_MODEL), lambda i: (i, j))
    return pl.pallas_call(
        _merge_body,
        grid=(t // MERGE_TM,),
        in_specs=[row(0), row(0), row(0), row(gates), row(gates + 1),
                  _resident(wpa.shape), _resident(wpb.shape), _resident(wout.shape),
                  _resident(fg.shape)],
        out_specs=row(0),
        out_shape=jax.ShapeDtypeStruct((t, D_MODEL), F32),
        compiler_params=pltpu.CompilerParams(
            dimension_semantics=("arbitrary",), vmem_limit_bytes=VMEM_LIMIT),
        name="merge",
    )(x2, ya, yb, main, main, wpa, wpb, wout, fg)


def _rope_tables(seq):
    inv = ROPE_THETA ** (-jnp.arange(0, QK_ROPE, 2, dtype=F32) / QK_ROPE)
    ang = jnp.arange(seq, dtype=F32)[:, None] * inv[None, :]
    cos, sin = jnp.cos(ang), jnp.sin(ang)
    zero = jnp.zeros_like(cos)
    cos_tab = jnp.concatenate([cos, cos, zero, zero], axis=-1)
    sin_tab = jnp.concatenate([-sin, sin, zero, zero], axis=-1)
    return cos_tab, sin_tab, cos_tab.T, sin_tab.T


def _pad_heads(w, width):
    rows, heads, d = w.shape
    return jnp.pad(w, ((0, 0), (0, 0), (0, width - d))).reshape(rows, heads * width)


def _layer(x2, batch, seq, norm_g, w_in, b_gate, lb, hg_norm_g, q_a_g, w_uq, kv_a_g, w_ukv,
           w_proj_a, w_proj_b, w_out, out_g, rope_tabs):
    half = QK_ROPE // 2
    o = 0
    w_hq, w_hf, w_hi, w_hz = (w_in[:, o + i * HG_WIDTH:o + (i + 1) * HG_WIDTH] for i in range(4))
    o += 4 * HG_WIDTH
    w_cq = w_in[:, o:o + Q_LORA]; o += Q_LORA
    w_ckv = w_in[:, o:o + KV_LORA]; o += KV_LORA
    w_kr = w_in[:, o:o + QK_ROPE]; o += QK_ROPE
    w_mz = w_in[:, o:o + MLA_WIDTH]; o += MLA_WIDTH
    w_gl = w_in[:, o:]
    kr_pad = jnp.zeros((D_MODEL, LANES - QK_ROPE), F32)
    w_kr_swapped = jnp.concatenate([w_kr[:, half:], w_kr[:, :half]], axis=1)
    w_main = jnp.concatenate([w_hq, w_hi, w_hz, w_mz, w_gl], axis=1).astype(BF16)
    w_mla = jnp.concatenate([w_cq, w_ckv, w_kr, kr_pad, w_kr_swapped, kr_pad], axis=1).astype(BF16)

    uq = w_uq.reshape(Q_LORA, MLA_HEADS, QK_DIM)
    q1, q2 = uq[:, :, QK_NOPE:QK_NOPE + half], uq[:, :, QK_NOPE + half:]
    wqa = _pad_heads(jnp.concatenate([uq[:, :, :QK_NOPE], q1, q2], axis=-1), QK_PAD).T.astype(BF16)
    wqb = _pad_heads(jnp.concatenate([q2, q1], axis=-1), LANES).T.astype(BF16)
    ukv = w_ukv.reshape(KV_LORA, MLA_HEADS, QK_NOPE + V_DIM)
    wkn = ukv[:, :, :QK_NOPE].reshape(KV_LORA, MLA_HEADS * QK_NOPE).astype(BF16)
    wv = ukv[:, :, QK_NOPE:].reshape(KV_LORA, MLA_WIDTH).T.astype(BF16)

    main, lf_hi, lf_lo, k_gate, mla = _proj(x2, norm_g[None], lb[None], b_gate[None], w_main,
                                            w_hf.astype(BF16), w_mla)
    y_a = _hgrn(main, lf_hi, lf_lo, k_gate, hg_norm_g[None], batch, seq)
    q_scale = QK_DIM ** -0.5 * math.log2(math.e)
    qt, k, vt = _mla_up(mla, q_a_g[None], kv_a_g[None], wqa, wqb, wkn, wv, rope_tabs,
                        batch, seq, q_scale)
    y_b = _attn(qt, k, vt, main, batch, seq)
    return _merge(x2, y_a, y_b, main, w_proj_a.astype(BF16), w_proj_b.astype(BF16),
                  w_out.astype(BF16), out_g[None])


def kernel(x, norm_g, w_in, b_gate, lb_logits, hg_norm_g, q_a_g, w_uq, kv_a_g, w_ukv,
           w_proj_a, w_proj_b, w_out, final_norm_g):
    batch, seq, _ = x.shape
    depth = norm_g.shape[0]
    assert depth == 1, "the final RMSNorm is fused into the single layer's merge kernel"
    lower_bounds = jnp.cumsum(jax.nn.softmax(lb_logits.astype(F32), axis=0), axis=0)[:depth]
    rope_tabs = _rope_tables(seq)
    x2 = x.reshape(batch * seq, D_MODEL)
    out = _layer(x2, batch, seq, norm_g[0], w_in[0], b_gate[0], lower_bounds[0], hg_norm_g[0],
                 q_a_g[0], w_uq[0], kv_a_g[0], w_ukv[0], w_proj_a[0], w_proj_b[0], w_out[0],
                 final_norm_g, rope_tabs)
    return out.reshape(batch, seq, D_MODEL)
```

```python
import functools
import math

import jax
import jax.numpy as jnp
from jax import lax
from jax.experimental import pallas as pl
from jax.experimental.pallas import tpu as pltpu

F32 = jnp.float32
BF16 = jnp.bfloat16

D_MODEL = 1024
HG_HEADS = 8
HG_DIM = 128
HG_WIDTH = HG_HEADS * HG_DIM
HG_CHUNK = 32
MLA_HEADS = 8
QK_NOPE = 128
QK_ROPE = 64
QK_DIM = QK_NOPE + QK_ROPE
V_DIM = 128
Q_LORA = 3 * D_MODEL // 8
KV_LORA = D_MODEL // 4
MLA_WIDTH = MLA_HEADS * V_DIM
ROPE_THETA = 10000.0
EPS = 1e-6

LANES = 128
QK_PAD = 2 * LANES
HG_GROUP = 128
VMEM_LIMIT = 56 * 1024 * 1024

PROJ_TM = 512
PROJ_CW = 512
HG_TC = 512
MLA_TS = 512
ATT_TQ = 2048
ATT_TK = 512
ATT_LOOKAHEAD = 2
ATT_SLOTS = ATT_TQ // ATT_TK
ATT_UNROLL = 2
BF16_ROWS = 16
V_EXT = V_DIM + BF16_ROWS
MERGE_TM = 512

MAIN_SECTIONS = ("q", "v", "z_a", "z_b", "gates", "gates")
MAIN_W = len(MAIN_SECTIONS) * D_MODEL
MLA_W = Q_LORA + KV_LORA + 2 * LANES


def _resident(shape):
    return pl.BlockSpec(shape, lambda *_: (0,) * len(shape), pipeline_mode=pl.Buffered(1))


def _rms(x, g):
    return x * lax.rsqrt(jnp.mean(x * x, axis=-1, keepdims=True) + EPS) * g


def _dot(a, b):
    return jnp.dot(a, b, preferred_element_type=F32)


def _dot_nt(a, b):
    return lax.dot_general(a, b, (((1,), (1,)), ((), ())), preferred_element_type=F32)


def _silu(x):
    return x * jax.nn.sigmoid(x)


def _proj_body(x_ref, g_ref, lb_ref, wmain_ref, whf_ref, wmla_ref,
               main_ref, lfh_ref, lfl_ref, k_ref, mla_ref):
    h = _rms(x_ref[...], g_ref[...]).astype(BF16)

    def main_chunk(c):
        cs = slice(c * PROJ_CW, (c + 1) * PROJ_CW)
        y = _dot(h, wmain_ref[:, cs])
        if MAIN_SECTIONS[c * PROJ_CW // D_MODEL] in ("q", "z_a"):
            y = _silu(y)
        main_ref[:, cs] = y.astype(BF16)

    def forget_chunk(c):
        cs = slice(c * PROJ_CW, (c + 1) * PROJ_CW)
        lb = lb_ref[:, cs]
        f = lb + (1.0 - lb) * jax.nn.sigmoid(_dot(h, whf_ref[:, cs]))
        log_f = jnp.log(f)
        lf_hi = log_f.astype(BF16)
        lfh_ref[:, cs] = lf_hi
        lfl_ref[:, cs] = (log_f - lf_hi.astype(F32)).astype(BF16)
        k_ref[:, cs] = (1.0 - f).astype(BF16)

    plain = [c for c in range(MAIN_W // PROJ_CW)
             if MAIN_SECTIONS[c * PROJ_CW // D_MODEL] not in ("q", "z_a")]
    heavy = ([("forget", c) for c in range(HG_WIDTH // PROJ_CW)]
             + [("main", c) for c in range(MAIN_W // PROJ_CW) if c not in plain])
    for kind, c in heavy:
        forget_chunk(c) if kind == "forget" else main_chunk(c)
        if plain:
            main_chunk(plain.pop(0))
    for c in plain:
        main_chunk(c)
    mla_ref[...] = _dot(h, wmla_ref[...]).astype(BF16)


def _proj(x2, norm_g, lb, w_main, w_hf, w_mla):
    t = x2.shape[0]
    row = lambda w: pl.BlockSpec((PROJ_TM, w), lambda i: (i, 0))
    out = lambda w: jax.ShapeDtypeStruct((t, w), BF16)
    return pl.pallas_call(
        _proj_body,
        grid=(t // PROJ_TM,),
        in_specs=[row(D_MODEL), _resident((1, D_MODEL)), _resident(lb.shape),
                  _resident(w_main.shape),
                  _resident(w_hf.shape), _resident(w_mla.shape)],
        out_specs=[row(MAIN_W), row(HG_WIDTH), row(HG_WIDTH), row(HG_WIDTH), row(MLA_W)],
        out_shape=[out(MAIN_W), out(HG_WIDTH), out(HG_WIDTH), out(HG_WIDTH), out(MLA_W)],
        compiler_params=pltpu.CompilerParams(
            dimension_semantics=("arbitrary",), vmem_limit_bytes=VMEM_LIMIT),
        name="proj",
    )(x2, norm_g, lb, w_main, w_hf, w_mla)


def _hgrn_body(q_ref, lfh_ref, lfl_ref, k_ref, v_ref, z_ref, g_ref, o_ref, st_ref):
    @pl.when(pl.program_id(1) == 0)
    def _():
        st_ref[...] = jnp.zeros_like(st_ref)

    n_chunks = HG_GROUP // HG_CHUNK
    shift = HG_CHUNK.bit_length() - 1
    r = lax.broadcasted_iota(jnp.int32, (HG_GROUP, HG_GROUP), 0)
    c = lax.broadcasted_iota(jnp.int32, (HG_GROUP, HG_GROUP), 1)
    col_chunk = lax.shift_right_logical(c, shift)
    causal = (lax.shift_right_logical(r, shift) == col_chunk) & (r >= c)
    prefix_mat = jnp.where(causal, 1.0, 0.0).astype(BF16)
    r_blk = lax.broadcasted_iota(jnp.int32, (HG_CHUNK, HG_GROUP), 0)
    c_blk = lax.broadcasted_iota(jnp.int32, (HG_CHUNK, HG_GROUP), 1)
    key_chunk_is = [lax.shift_right_logical(c_blk, shift) == kc for kc in range(n_chunks)]
    causal_blk = [key_chunk_is[qc] & (c_blk <= r_blk + qc * HG_CHUNK) for qc in range(n_chunks)]
    pairs = [(kc + gap, kc) for gap in range(1, n_chunks) for kc in range(n_chunks - gap)]

    def chunk_rows(ci):
        return slice(ci * HG_CHUNK, (ci + 1) * HG_CHUNK)

    def by_chunk(fn, x):
        return jnp.concatenate([fn(ci, x[chunk_rows(ci)]) for ci in range(n_chunks)], axis=0)

    def group(gi, carry):
        rows = pl.ds(pl.multiple_of(gi * HG_GROUP, HG_GROUP), HG_GROUP)
        b_all = (_dot(prefix_mat, lfh_ref[rows, :])
                 + _dot(prefix_mat, lfl_ref[rows, :]))

        partial = []
        for h in range(HG_HEADS):
            cols = slice(h * HG_DIM, (h + 1) * HG_DIM)
            b = b_all[:, cols]
            k = k_ref[rows, cols].astype(F32)
            tot = [b[(ci + 1) * HG_CHUNK - 1:(ci + 1) * HG_CHUNK] for ci in range(n_chunks)]
            zero = jnp.zeros_like(tot[0])
            before = [zero]
            for ci in range(1, n_chunks):
                before.append(before[-1] + tot[ci - 1])
            after = [zero]
            for ci in range(n_chunks - 2, -1, -1):
                after.insert(0, after[0] + tot[ci + 1])
            q_in = q_ref[rows, cols].astype(F32) * jnp.exp(b)
            k_in = (k * jnp.exp(-b)).astype(BF16)
            k_out = k * jnp.exp(by_chunk(lambda ci, x: tot[ci] - x, b))
            q_start = by_chunk(lambda ci, x: x * jnp.exp(before[ci]), q_in).astype(BF16)
            k_end = by_chunk(lambda ci, x: x * jnp.exp(after[ci]), k_out).astype(BF16)
            q_cross = []
            for qc, kc in pairs:
                q_blk = q_in[chunk_rows(qc)]
                if qc > kc + 1:
                    q_blk = q_blk * jnp.exp(before[qc] - before[kc + 1])
                q_cross.append(q_blk)
            q_cross = jnp.concatenate(q_cross, axis=0).astype(BF16)

            v = v_ref[rows, cols]
            v_t = v.astype(F32).T.astype(BF16)
            st = st_ref[h]
            same = _dot_nt(q_in.astype(BF16), k_in)
            cross = _dot_nt(q_cross, k_out.astype(BF16))
            o_start = _dot_nt(q_start, st.astype(BF16))
            st_ref[h] = st * jnp.exp(before[-1] + tot[-1]) + _dot(v_t, k_end)
            partial.append((same, cross, o_start, v))

        for h in range(HG_HEADS):
            cols = slice(h * HG_DIM, (h + 1) * HG_DIM)
            same, cross, o_start, v = partial[h]
            score_rows = []
            for qc in range(n_chunks):
                blk = jnp.where(causal_blk[qc], same[chunk_rows(qc)], 0.0)
                for idx, (pq, pk) in enumerate(pairs):
                    if pq == qc:
                        blk = jnp.where(key_chunk_is[pk], cross[chunk_rows(idx)], blk)
                score_rows.append(blk)
            scores = jnp.concatenate(score_rows, axis=0).astype(BF16)
            o = _dot(scores, v) + o_start
            y = _rms(o, g_ref[...]) * z_ref[rows, cols].astype(F32)
            o_ref[rows, cols] = y.astype(BF16)
        return carry

    lax.fori_loop(0, HG_TC // HG_GROUP, group, 0)


def _hgrn(main, lf_hi, lf_lo, k, hg_g, batch, seq):
    nblk = seq // HG_TC
    t = batch * seq

    def col(name=None):
        j = 0 if name is None else MAIN_SECTIONS.index(name)
        return pl.BlockSpec((HG_TC, HG_WIDTH), lambda b, s: (b * nblk + s, j))

    return pl.pallas_call(
        _hgrn_body,
        grid=(batch, nblk),
        in_specs=[col("q"), col(), col(), col(), col("v"), col("z_a"), _resident((1, HG_DIM))],
        out_specs=col(),
        out_shape=jax.ShapeDtypeStruct((t, HG_WIDTH), BF16),
        scratch_shapes=[pltpu.VMEM((HG_HEADS, HG_DIM, HG_DIM), F32)],
        compiler_params=pltpu.CompilerParams(
            dimension_semantics=("arbitrary", "arbitrary"), vmem_limit_bytes=VMEM_LIMIT),
        name="hgrn2",
    )(main, lf_hi, lf_lo, k, main, main, hg_g)


def _mla_up_body(mla_ref, qg_ref, kvg_ref, wqa_ref, wqb_ref, wkn_ref, wv_ref,
                 cos_ref, sin_ref, cos_t_ref, sin_t_ref, qt_ref, k_ref, vt_ref, *, q_scale):
    lat = mla_ref[...].astype(F32)
    cq = _rms(lat[:, :Q_LORA], qg_ref[...]).astype(BF16)
    ckv = _rms(lat[:, Q_LORA:Q_LORA + KV_LORA], kvg_ref[...]).astype(BF16)
    kr_a = lat[:, Q_LORA + KV_LORA:Q_LORA + KV_LORA + LANES]
    kr_b = lat[:, Q_LORA + KV_LORA + LANES:]
    k_pe = (kr_a * cos_ref[...] + kr_b * sin_ref[...]).astype(BF16)
    k_nope = _dot(ckv, wkn_ref[...])
    qa_t = _dot_nt(wqa_ref[...], cq)
    qb_t = _dot_nt(wqb_ref[...], cq)
    v_t = _dot_nt(wv_ref[...], ckv)
    cos_t = cos_t_ref[...]
    sin_t = sin_t_ref[...]
    ones_row = jnp.where(lax.broadcasted_iota(jnp.int32, (BF16_ROWS, ATT_TK), 0) == 0,
                         1.0, 0.0).astype(BF16)
    q_zero = jnp.zeros((QK_PAD - QK_DIM, MLA_TS), BF16)
    for h in range(MLA_HEADS):
        hs = slice(h * LANES, (h + 1) * LANES)
        q_h = qa_t[h * QK_DIM:(h + 1) * QK_DIM]
        qt_ref[h, :QK_NOPE, :] = (q_h[:QK_NOPE] * q_scale).astype(BF16)
        q_pe = q_h[QK_NOPE:] * cos_t + qb_t[h * QK_ROPE:(h + 1) * QK_ROPE] * sin_t
        qt_ref[h, QK_NOPE:QK_DIM, :] = (q_pe * q_scale).astype(BF16)
        qt_ref[h, QK_DIM:, :] = q_zero
        k_ref[h, :, :LANES] = k_nope[:, hs].astype(BF16)
        k_ref[h, :, LANES:] = k_pe
        for c in range(MLA_TS // ATT_TK):
            vt_ref[h, c, :V_DIM, :] = v_t[hs, c * ATT_TK:(c + 1) * ATT_TK].astype(BF16)
            vt_ref[h, c, V_DIM:, :] = ones_row


def _mla_up(mla, qg, kvg, wqa, wqb, wkn, wv, tabs, batch, seq, q_scale):
    nblk = seq // MLA_TS
    kv_per_step = MLA_TS // ATT_TK
    tab = pl.BlockSpec((MLA_TS, LANES), lambda b, s: (s, 0))
    tab_t = pl.BlockSpec((QK_ROPE, MLA_TS), lambda b, s: (0, s))
    return pl.pallas_call(
        functools.partial(_mla_up_body, q_scale=q_scale),
        grid=(batch, nblk),
        in_specs=[pl.BlockSpec((MLA_TS, MLA_W), lambda b, s: (b * nblk + s, 0)),
                  _resident(qg.shape), _resident(kvg.shape), _resident(wqa.shape),
                  _resident(wqb.shape), _resident(wkn.shape), _resident(wv.shape),
                  tab, tab, tab_t, tab_t],
        out_specs=[pl.BlockSpec((MLA_HEADS, QK_PAD, MLA_TS), lambda b, s: (b, 0, s)),
                   pl.BlockSpec((MLA_HEADS, MLA_TS, QK_PAD), lambda b, s: (b, s, 0)),
                   pl.BlockSpec((MLA_HEADS, kv_per_step, V_EXT, ATT_TK),
                                lambda b, s: (b, s, 0, 0))],
        out_shape=[jax.ShapeDtypeStruct((batch * MLA_HEADS, QK_PAD, seq), BF16),
                   jax.ShapeDtypeStruct((batch * MLA_HEADS, seq, QK_PAD), BF16),
                   jax.ShapeDtypeStruct((batch * MLA_HEADS, seq // ATT_TK, V_EXT, ATT_TK), BF16)],
        compiler_params=pltpu.CompilerParams(
            dimension_semantics=("arbitrary", "arbitrary"), vmem_limit_bytes=VMEM_LIMIT),
        name="mla_up",
    )(mla, qg, kvg, wqa, wqb, wkn, wv, *tabs)


def _attn_body(qt_ref, k_ref, vt_ref, z_ref, o_ref, s_ref, smax_ref, acc_ref, m_ref):
    qi = pl.program_id(2)
    n_strips = ATT_TQ // ATT_TK

    def lanes_of(si):
        return slice(si * ATT_TK, (si + 1) * ATT_TK)

    def scores(item, slot):
        j, si, mask = item
        ks = pl.ds(pl.multiple_of(j * ATT_TK, ATT_TK), ATT_TK)
        s = _dot(k_ref[0, ks, :], qt_ref[0, :, lanes_of(si)])
        if mask is not None:
            key = lax.broadcasted_iota(jnp.int32, s.shape, 0)
            query = lax.broadcasted_iota(jnp.int32, s.shape, 1)
            if mask == "positions":
                query = query + ((qi * n_strips + si - j) * ATT_TK)
            s = jnp.where(query >= key, s, -jnp.inf)
        s_ref[slot] = s
        smax_ref[slot] = jnp.max(s, axis=0, keepdims=True)

    def softmax_pv(item, slot):
        j, si, _ = item
        lanes = lanes_of(si)
        m_prev = m_ref[:, lanes]
        m_new = jnp.maximum(m_prev, smax_ref[slot])
        alpha = jnp.exp2(m_prev - m_new)
        p = jnp.exp2(s_ref[slot] - m_new).astype(BF16)
        acc_ref[:, lanes] = alpha * acc_ref[:, lanes] + _dot(vt_ref[0, j], p)
        m_ref[:, lanes] = m_new

    def run(items, n_consume):
        for idx in range(n_consume):
            ahead = idx + ATT_LOOKAHEAD
            if ahead < len(items):
                scores(items[ahead], ahead % ATT_SLOTS)
            softmax_pv(items[idx], idx % ATT_SLOTS)

    def key_blocks(t, carry):
        j = t * ATT_UNROLL
        items = [(j + b, si, None) for b in range(ATT_UNROLL) for si in range(n_strips)]
        items += [(j + ATT_UNROLL, si, "positions" if si == 0 else None)
                  for si in range(n_strips)]
        run(items, ATT_UNROLL * n_strips)
        return carry

    first = qi * n_strips
    diagonal = [(first + c, si, "diagonal" if si == c else None)
                for c in range(n_strips) for si in range(c, n_strips)]

    acc_ref[...] = jnp.zeros_like(acc_ref)
    m_ref[...] = jnp.full_like(m_ref, -jnp.inf)
    for idx in range(ATT_LOOKAHEAD):
        scores((0, idx, "positions" if idx == 0 else None), idx)
    lax.fori_loop(0, qi * (n_strips // ATT_UNROLL), key_blocks, 0)
    run(diagonal, len(diagonal))

    o_t = acc_ref[:V_DIM] * (1.0 / acc_ref[V_DIM:V_DIM + 1])
    o_ref[...] = (o_t.T * _silu(z_ref[...].astype(F32))).astype(BF16)


def _attn(qt, k, vt, main, batch, seq):
    assert ATT_TQ % ATT_TK == 0
    nq = seq // ATT_TQ
    z_col0 = MAIN_SECTIONS.index("z_b") * D_MODEL // V_DIM
    head = lambda b, h, i: b * MLA_HEADS + h
    return pl.pallas_call(
        _attn_body,
        grid=(batch, MLA_HEADS, nq),
        in_specs=[pl.BlockSpec((1, QK_PAD, ATT_TQ), lambda b, h, i: (head(b, h, i), 0, i)),
                  pl.BlockSpec((1, seq, QK_PAD), lambda b, h, i: (head(b, h, i), 0, 0)),
                  pl.BlockSpec((1, seq // ATT_TK, V_EXT, ATT_TK),
                               lambda b, h, i: (head(b, h, i), 0, 0, 0)),
                  pl.BlockSpec((ATT_TQ, V_DIM), lambda b, h, i: (b * nq + i, z_col0 + h))],
        out_specs=pl.BlockSpec((ATT_TQ, V_DIM), lambda b, h, i: (b * nq + i, h)),
        out_shape=jax.ShapeDtypeStruct((batch * seq, MLA_WIDTH), BF16),
        scratch_shapes=[pltpu.VMEM((ATT_SLOTS, ATT_TK, ATT_TK), F32),
                        pltpu.VMEM((ATT_SLOTS, 1, ATT_TK), F32),
                        pltpu.VMEM((V_EXT, ATT_TQ), F32), pltpu.VMEM((1, ATT_TQ), F32)],
        compiler_params=pltpu.CompilerParams(
            dimension_semantics=("arbitrary", "arbitrary", "arbitrary"),
            vmem_limit_bytes=VMEM_LIMIT),
        name="attn",
    )(qt, k, vt, main)


def _merge_body(x_ref, ya_ref, yb_ref, ga_ref, gb_ref, bg_ref, wpa_ref, wpb_ref, wout_ref,
                fg_ref, o_ref):
    gate_a = jax.nn.sigmoid(ga_ref[...].astype(F32) + bg_ref[:, :D_MODEL])
    gate_b = jax.nn.sigmoid(gb_ref[...].astype(F32) + bg_ref[:, D_MODEL:])
    merged = gate_a * _dot(ya_ref[...], wpa_ref[...]) + gate_b * _dot(yb_ref[...], wpb_ref[...])
    x_new = x_ref[...] + _dot(merged.astype(BF16), wout_ref[...])
    o_ref[...] = _rms(x_new, fg_ref[...])


def _merge(x2, ya, yb, main, b_gate, wpa, wpb, wout, fg):
    t = x2.shape[0]
    gates = MAIN_SECTIONS.index("gates")
    row = lambda j: pl.BlockSpec((MERGE_TM, D_MODEL), lambda i: (i, j))
    return pl.pallas_call(
        _merge_body,
        grid=(t // MERGE_TM,),
        in_specs=[row(0), row(0), row(0), row(gates), row(gates + 1), _resident(b_gate.shape),
                  _resident(wpa.shape), _resident(wpb.shape), _resident(wout.shape),
                  _resident(fg.shape)],
        out_specs=row(0),
        out_shape=jax.ShapeDtypeStruct((t, D_MODEL), F32),
        compiler_params=pltpu.CompilerParams(
            dimension_semantics=("arbitrary",), vmem_limit_bytes=VMEM_LIMIT),
        name="merge",
    )(x2, ya, yb, main, main, b_gate, wpa, wpb, wout, fg)


def _rope_tables(seq):
    inv = ROPE_THETA ** (-jnp.arange(0, QK_ROPE, 2, dtype=F32) / QK_ROPE)
    ang = jnp.arange(seq, dtype=F32)[:, None] * inv[None, :]
    cos, sin = jnp.cos(ang), jnp.sin(ang)
    zero = jnp.zeros_like(cos)
    cos_tab = jnp.concatenate([cos, cos, zero, zero], axis=-1)
    sin_tab = jnp.concatenate([-sin, sin, zero, zero], axis=-1)
    return cos_tab, sin_tab, cos_tab[:, :QK_ROPE].T, sin_tab[:, :QK_ROPE].T


def _layer(x2, batch, seq, norm_g, w_in, b_gate, lb, hg_norm_g, q_a_g, w_uq, kv_a_g, w_ukv,
           w_proj_a, w_proj_b, w_out, out_g, rope_tabs):
    half = QK_ROPE // 2
    o = 0
    w_hq, w_hf, w_hi, w_hz = (w_in[:, o + i * HG_WIDTH:o + (i + 1) * HG_WIDTH] for i in range(4))
    o += 4 * HG_WIDTH
    w_cq = w_in[:, o:o + Q_LORA]; o += Q_LORA
    w_ckv = w_in[:, o:o + KV_LORA]; o += KV_LORA
    w_kr = w_in[:, o:o + QK_ROPE]; o += QK_ROPE
    w_mz = w_in[:, o:o + MLA_WIDTH]; o += MLA_WIDTH
    w_gl = w_in[:, o:]
    kr_pad = jnp.zeros((D_MODEL, LANES - QK_ROPE), F32)
    w_kr_swapped = jnp.concatenate([w_kr[:, half:], w_kr[:, :half]], axis=1)
    w_main = jnp.concatenate([w_hq, w_hi, w_hz, w_mz, w_gl], axis=1).astype(BF16)
    w_mla = jnp.concatenate([w_cq, w_ckv, w_kr, kr_pad, w_kr_swapped, kr_pad], axis=1).astype(BF16)

    uq = w_uq.reshape(Q_LORA, MLA_HEADS, QK_DIM)
    q1, q2 = uq[:, :, QK_NOPE:QK_NOPE + half], uq[:, :, QK_NOPE + half:]
    wqa = uq.reshape(Q_LORA, MLA_HEADS * QK_DIM).T.astype(BF16)
    wqb = jnp.concatenate([q2, q1], axis=-1).reshape(Q_LORA, MLA_HEADS * QK_ROPE).T.astype(BF16)
    ukv = w_ukv.reshape(KV_LORA, MLA_HEADS, QK_NOPE + V_DIM)
    wkn = ukv[:, :, :QK_NOPE].reshape(KV_LORA, MLA_HEADS * QK_NOPE).astype(BF16)
    wv = ukv[:, :, QK_NOPE:].reshape(KV_LORA, MLA_WIDTH).T.astype(BF16)

    main, lf_hi, lf_lo, k_gate, mla = _proj(x2, norm_g[None], lb[None], w_main,
                                            w_hf.astype(BF16), w_mla)
    y_a = _hgrn(main, lf_hi, lf_lo, k_gate, hg_norm_g[None], batch, seq)
    q_scale = QK_DIM ** -0.5 * math.log2(math.e)
    qt, k, vt = _mla_up(mla, q_a_g[None], kv_a_g[None], wqa, wqb, wkn, wv, rope_tabs,
                        batch, seq, q_scale)
    y_b = _attn(qt, k, vt, main, batch, seq)
    return _merge(x2, y_a, y_b, main, b_gate[None], w_proj_a.astype(BF16),
                  w_proj_b.astype(BF16), w_out.astype(BF16), out_g[None])


def kernel(x, norm_g, w_in, b_gate, lb_logits, hg_norm_g, q_a_g, w_uq, kv_a_g, w_ukv,
           w_proj_a, w_proj_b, w_out, final_norm_g):
    batch, seq, _ = x.shape
    depth = norm_g.shape[0]
    assert depth == 1, "the final RMSNorm is fused into the single layer's merge kernel"
    lower_bounds = jnp.cumsum(jax.nn.softmax(lb_logits.astype(F32), axis=0), axis=0)[:depth]
    rope_tabs = _rope_tables(seq)
    x2 = x.reshape(batch * seq, D_MODEL)
    out = _layer(x2, batch, seq, norm_g[0], w_in[0], b_gate[0], lower_bounds[0], hg_norm_g[0],
                 q_a_g[0], w_uq[0], kv_a_g[0], w_ukv[0], w_proj_a[0], w_proj_b[0], w_out[0],
                 final_norm_g, rope_tabs)
    return out.reshape(batch, seq, D_MODEL)
```

```python
import functools
import math

import jax
import jax.numpy as jnp
from jax import lax
from jax.experimental import pallas as pl
from jax.experimental.pallas import tpu as pltpu

F32 = jnp.float32
BF16 = jnp.bfloat16

D_MODEL = 1024
HG_HEADS = 8
HG_DIM = 128
HG_WIDTH = HG_HEADS * HG_DIM
HG_CHUNK = 32
MLA_HEADS = 8
QK_NOPE = 128
QK_ROPE = 64
QK_DIM = QK_NOPE + QK_ROPE
V_DIM = 128
Q_LORA = 3 * D_MODEL // 8
KV_LORA = D_MODEL // 4
MLA_WIDTH = MLA_HEADS * V_DIM
ROPE_THETA = 10000.0
EPS = 1e-6

LANES = 128
QK_PAD = 2 * LANES
HG_GROUP = 128
VMEM_LIMIT = 56 * 1024 * 1024

PROJ_TM = 512
PROJ_CW = 512
HG_TC = 512
MLA_TS = 512
ATT_TQ = 2048
ATT_TK = 512
ATT_LOOKAHEAD = 2
ATT_SLOTS = ATT_TQ // ATT_TK
ATT_UNROLL = 4
BF16_ROWS = 16
V_EXT = V_DIM + BF16_ROWS
MERGE_TM = 512

MAIN_SECTIONS = ("q", "v", "z_a", "z_b", "gates", "gates")
MAIN_W = len(MAIN_SECTIONS) * D_MODEL
MLA_W = Q_LORA + KV_LORA + 2 * LANES


def _resident(shape):
    return pl.BlockSpec(shape, lambda *_: (0,) * len(shape), pipeline_mode=pl.Buffered(1))


def _rms(x, g):
    return x * lax.rsqrt(jnp.mean(x * x, axis=-1, keepdims=True) + EPS) * g


def _dot(a, b):
    return jnp.dot(a, b, preferred_element_type=F32)


def _dot_nt(a, b):
    return lax.dot_general(a, b, (((1,), (1,)), ((), ())), preferred_element_type=F32)


def _sigmoid(x):
    return 0.5 * jnp.tanh(0.5 * x) + 0.5


def _silu(x):
    half = 0.5 * x
    return half + half * jnp.tanh(half)


def _proj_body(x_ref, g_ref, lb_ref, wmain_ref, whf_ref, wmla_ref,
               main_ref, lfh_ref, lfl_ref, k_ref, mla_ref):
    h = _rms(x_ref[...], g_ref[...]).astype(BF16)

    def main_chunk(c):
        cs = slice(c * PROJ_CW, (c + 1) * PROJ_CW)
        y = _dot(h, wmain_ref[:, cs])
        if MAIN_SECTIONS[c * PROJ_CW // D_MODEL] in ("q", "z_a"):
            y = _silu(y)
        main_ref[:, cs] = y.astype(BF16)

    def forget_chunk(c):
        cs = slice(c * PROJ_CW, (c + 1) * PROJ_CW)
        lb = lb_ref[:, cs]
        f = lb + (1.0 - lb) * _sigmoid(_dot(h, whf_ref[:, cs]))
        log_f = jnp.log(f)
        lf_hi = log_f.astype(BF16)
        lfh_ref[:, cs] = lf_hi
        lfl_ref[:, cs] = (log_f - lf_hi.astype(F32)).astype(BF16)
        k_ref[:, cs] = (1.0 - f).astype(BF16)

    plain = [c for c in range(MAIN_W // PROJ_CW)
             if MAIN_SECTIONS[c * PROJ_CW // D_MODEL] not in ("q", "z_a")]
    heavy = ([("forget", c) for c in range(HG_WIDTH // PROJ_CW)]
             + [("main", c) for c in range(MAIN_W // PROJ_CW) if c not in plain])
    for kind, c in heavy:
        forget_chunk(c) if kind == "forget" else main_chunk(c)
        if plain:
            main_chunk(plain.pop(0))
    for c in plain:
        main_chunk(c)
    mla_ref[...] = _dot(h, wmla_ref[...]).astype(BF16)


def _proj(x2, norm_g, lb, w_main, w_hf, w_mla):
    t = x2.shape[0]
    row = lambda w: pl.BlockSpec((PROJ_TM, w), lambda i: (i, 0))
    out = lambda w: jax.ShapeDtypeStruct((t, w), BF16)
    return pl.pallas_call(
        _proj_body,
        grid=(t // PROJ_TM,),
        in_specs=[row(D_MODEL), _resident((1, D_MODEL)), _resident(lb.shape),
                  _resident(w_main.shape),
                  _resident(w_hf.shape), _resident(w_mla.shape)],
        out_specs=[row(MAIN_W), row(HG_WIDTH), row(HG_WIDTH), row(HG_WIDTH), row(MLA_W)],
        out_shape=[out(MAIN_W), out(HG_WIDTH), out(HG_WIDTH), out(HG_WIDTH), out(MLA_W)],
        compiler_params=pltpu.CompilerParams(
            dimension_semantics=("arbitrary",), vmem_limit_bytes=VMEM_LIMIT),
        name="proj",
    )(x2, norm_g, lb, w_main, w_hf, w_mla)


def _hgrn_body(q_ref, lfh_ref, lfl_ref, k_ref, v_ref, z_ref, g_ref, o_ref, st_ref):
    @pl.when(pl.program_id(1) == 0)
    def _():
        st_ref[...] = jnp.zeros_like(st_ref)

    n_chunks = HG_GROUP // HG_CHUNK
    shift = HG_CHUNK.bit_length() - 1
    r = lax.broadcasted_iota(jnp.int32, (HG_GROUP, HG_GROUP), 0)
    c = lax.broadcasted_iota(jnp.int32, (HG_GROUP, HG_GROUP), 1)
    col_chunk = lax.shift_right_logical(c, shift)
    causal = (lax.shift_right_logical(r, shift) == col_chunk) & (r >= c)
    prefix_mat = jnp.where(causal, 1.0, 0.0).astype(BF16)
    r_blk = lax.broadcasted_iota(jnp.int32, (HG_CHUNK, HG_GROUP), 0)
    c_blk = lax.broadcasted_iota(jnp.int32, (HG_CHUNK, HG_GROUP), 1)
    key_chunk_is = [lax.shift_right_logical(c_blk, shift) == kc for kc in range(n_chunks)]
    causal_blk = [key_chunk_is[qc] & (c_blk <= r_blk + qc * HG_CHUNK) for qc in range(n_chunks)]
    pairs = [(kc + gap, kc) for gap in range(1, n_chunks) for kc in range(n_chunks - gap)]

    def chunk_rows(ci):
        return slice(ci * HG_CHUNK, (ci + 1) * HG_CHUNK)

    def by_chunk(fn, x):
        return jnp.concatenate([fn(ci, x[chunk_rows(ci)]) for ci in range(n_chunks)], axis=0)

    def group(gi, carry):
        rows = pl.ds(pl.multiple_of(gi * HG_GROUP, HG_GROUP), HG_GROUP)
        b_all = (_dot(prefix_mat, lfh_ref[rows, :])
                 + _dot(prefix_mat, lfl_ref[rows, :]))

        partial = []
        for h in range(HG_HEADS):
            cols = slice(h * HG_DIM, (h + 1) * HG_DIM)
            b = b_all[:, cols]
            k = k_ref[rows, cols].astype(F32)
            tot = [b[(ci + 1) * HG_CHUNK - 1:(ci + 1) * HG_CHUNK] for ci in range(n_chunks)]
            zero = jnp.zeros_like(tot[0])
            before = [zero]
            for ci in range(1, n_chunks):
                before.append(before[-1] + tot[ci - 1])
            after = [zero]
            for ci in range(n_chunks - 2, -1, -1):
                after.insert(0, after[0] + tot[ci + 1])
            q_in = q_ref[rows, cols].astype(F32) * jnp.exp(b)
            k_in = (k * jnp.exp(-b)).astype(BF16)
            k_out = k * jnp.exp(by_chunk(lambda ci, x: tot[ci] - x, b))
            q_start = by_chunk(lambda ci, x: x * jnp.exp(before[ci]), q_in).astype(BF16)
            k_end = by_chunk(lambda ci, x: x * jnp.exp(after[ci]), k_out).astype(BF16)
            q_cross = []
            for qc, kc in pairs:
                q_blk = q_in[chunk_rows(qc)]
                if qc > kc + 1:
                    q_blk = q_blk * jnp.exp(before[qc] - before[kc + 1])
                q_cross.append(q_blk)
            q_cross = jnp.concatenate(q_cross, axis=0).astype(BF16)

            v = v_ref[rows, cols]
            v_t = v.astype(F32).T.astype(BF16)
            st = st_ref[h]
            same = _dot_nt(q_in.astype(BF16), k_in)
            cross = _dot_nt(q_cross, k_out.astype(BF16))
            o_start = _dot_nt(q_start, st.astype(BF16))
            st_ref[h] = st * jnp.exp(before[-1] + tot[-1]) + _dot(v_t, k_end)
            partial.append((same, cross, o_start, v))

        for h in range(HG_HEADS):
            cols = slice(h * HG_DIM, (h + 1) * HG_DIM)
            same, cross, o_start, v = partial[h]
            score_rows = []
            for qc in range(n_chunks):
                blk = jnp.where(causal_blk[qc], same[chunk_rows(qc)], 0.0)
                for idx, (pq, pk) in enumerate(pairs):
                    if pq == qc:
                        blk = jnp.where(key_chunk_is[pk], cross[chunk_rows(idx)], blk)
                score_rows.append(blk)
            scores = jnp.concatenate(score_rows, axis=0).astype(BF16)
            o = _dot(scores, v) + o_start
            y = _rms(o, g_ref[...]) * z_ref[rows, cols].astype(F32)
            o_ref[rows, cols] = y.astype(BF16)
        return carry

    lax.fori_loop(0, HG_TC // HG_GROUP, group, 0, unroll=True)


def _hgrn(main, lf_hi, lf_lo, k, hg_g, batch, seq):
    nblk = seq // HG_TC
    t = batch * seq

    def col(name=None):
        j = 0 if name is None else MAIN_SECTIONS.index(name)
        return pl.BlockSpec((HG_TC, HG_WIDTH), lambda b, s: (b * nblk + s, j))

    return pl.pallas_call(
        _hgrn_body,
        grid=(batch, nblk),
        in_specs=[col("q"), col(), col(), col(), col("v"), col("z_a"), _resident((1, HG_DIM))],
        out_specs=col(),
        out_shape=jax.ShapeDtypeStruct((t, HG_WIDTH), BF16),
        scratch_shapes=[pltpu.VMEM((HG_HEADS, HG_DIM, HG_DIM), F32)],
        compiler_params=pltpu.CompilerParams(
            dimension_semantics=("arbitrary", "arbitrary"), vmem_limit_bytes=VMEM_LIMIT),
        name="hgrn2",
    )(main, lf_hi, lf_lo, k, main, main, hg_g)


def _mla_up_body(mla_ref, qg_ref, kvg_ref, wqa_ref, wqb_ref, wkn_ref, wv_ref,
                 cos_ref, sin_ref, cos_t_ref, sin_t_ref, qt_ref, k_ref, vt_ref, *, q_scale):
    lat = mla_ref[...].astype(F32)
    cq = _rms(lat[:, :Q_LORA], qg_ref[...]).astype(BF16)
    ckv = _rms(lat[:, Q_LORA:Q_LORA + KV_LORA], kvg_ref[...]).astype(BF16)
    kr_a = lat[:, Q_LORA + KV_LORA:Q_LORA + KV_LORA + LANES]
    kr_b = lat[:, Q_LORA + KV_LORA + LANES:]
    k_pe = (kr_a * cos_ref[...] + kr_b * sin_ref[...]).astype(BF16)
    k_nope = _dot(ckv, wkn_ref[...])
    qa_t = _dot_nt(wqa_ref[...], cq)
    qb_t = _dot_nt(wqb_ref[...], cq)
    v_t = _dot_nt(wv_ref[...], ckv)
    cos_t = cos_t_ref[...]
    sin_t = sin_t_ref[...]
    ones_row = jnp.where(lax.broadcasted_iota(jnp.int32, (BF16_ROWS, ATT_TK), 0) == 0,
                         1.0, 0.0).astype(BF16)
    q_zero = jnp.zeros((QK_PAD - QK_DIM, MLA_TS), BF16)
    for h in range(MLA_HEADS):
        hs = slice(h * LANES, (h + 1) * LANES)
        q_h = qa_t[h * QK_DIM:(h + 1) * QK_DIM]
        qt_ref[h, :QK_NOPE, :] = (q_h[:QK_NOPE] * q_scale).astype(BF16)
        q_pe = q_h[QK_NOPE:] * cos_t + qb_t[h * QK_ROPE:(h + 1) * QK_ROPE] * sin_t
        qt_ref[h, QK_NOPE:QK_DIM, :] = (q_pe * q_scale).astype(BF16)
        qt_ref[h, QK_DIM:, :] = q_zero
        k_ref[h, :, :LANES] = k_nope[:, hs].astype(BF16)
        k_ref[h, :, LANES:] = k_pe
        for c in range(MLA_TS // ATT_TK):
            vt_ref[h, c, :V_DIM, :] = v_t[hs, c * ATT_TK:(c + 1) * ATT_TK].astype(BF16)
            vt_ref[h, c, V_DIM:, :] = ones_row


def _mla_up(mla, qg, kvg, wqa, wqb, wkn, wv, tabs, batch, seq, q_scale):
    nblk = seq // MLA_TS
    kv_per_step = MLA_TS // ATT_TK
    tab = pl.BlockSpec((MLA_TS, LANES), lambda b, s: (s, 0))
    tab_t = pl.BlockSpec((QK_ROPE, MLA_TS), lambda b, s: (0, s))
    return pl.pallas_call(
        functools.partial(_mla_up_body, q_scale=q_scale),
        grid=(batch, nblk),
        in_specs=[pl.BlockSpec((MLA_TS, MLA_W), lambda b, s: (b * nblk + s, 0)),
                  _resident(qg.shape), _resident(kvg.shape), _resident(wqa.shape),
                  _resident(wqb.shape), _resident(wkn.shape), _resident(wv.shape),
                  tab, tab, tab_t, tab_t],
        out_specs=[pl.BlockSpec((MLA_HEADS, QK_PAD, MLA_TS), lambda b, s: (b, 0, s)),
                   pl.BlockSpec((MLA_HEADS, MLA_TS, QK_PAD), lambda b, s: (b, s, 0)),
                   pl.BlockSpec((MLA_HEADS, kv_per_step, V_EXT, ATT_TK),
                                lambda b, s: (b, s, 0, 0))],
        out_shape=[jax.ShapeDtypeStruct((batch * MLA_HEADS, QK_PAD, seq), BF16),
                   jax.ShapeDtypeStruct((batch * MLA_HEADS, seq, QK_PAD), BF16),
                   jax.ShapeDtypeStruct((batch * MLA_HEADS, seq // ATT_TK, V_EXT, ATT_TK), BF16)],
        compiler_params=pltpu.CompilerParams(
            dimension_semantics=("arbitrary", "arbitrary"), vmem_limit_bytes=VMEM_LIMIT),
        name="mla_up",
    )(mla, qg, kvg, wqa, wqb, wkn, wv, *tabs)


def _attn_body(qt_ref, k_ref, vt_ref, z_ref, o_ref, s_ref, smax_ref, acc_ref, m_ref):
    qi = pl.program_id(2)
    n_strips = ATT_TQ // ATT_TK

    def lanes_of(si):
        return slice(si * ATT_TK, (si + 1) * ATT_TK)

    def scores(item, slot):
        j, si, mask = item
        ks = pl.ds(pl.multiple_of(j * ATT_TK, ATT_TK), ATT_TK)
        s = _dot(k_ref[0, ks, :], qt_ref[0, :, lanes_of(si)])
        if mask is not None:
            key = lax.broadcasted_iota(jnp.int32, s.shape, 0)
            query = lax.broadcasted_iota(jnp.int32, s.shape, 1)
            if mask == "positions":
                query = query + ((qi * n_strips + si - j) * ATT_TK)
            s = jnp.where(query >= key, s, -jnp.inf)
        s_ref[slot] = s
        smax_ref[slot] = jnp.max(s, axis=0, keepdims=True)

    def softmax_pv(item, slot):
        j, si, _ = item
        lanes = lanes_of(si)
        m_prev = m_ref[:, lanes]
        m_new = jnp.maximum(m_prev, smax_ref[slot])
        alpha = jnp.exp2(m_prev - m_new)
        p = jnp.exp2(s_ref[slot] - m_new).astype(BF16)
        acc_ref[:, lanes] = alpha * acc_ref[:, lanes] + _dot(vt_ref[0, j], p)
        m_ref[:, lanes] = m_new

    def run(items, n_consume):
        for idx in range(n_consume):
            ahead = idx + ATT_LOOKAHEAD
            if ahead < len(items):
                scores(items[ahead], ahead % ATT_SLOTS)
            softmax_pv(items[idx], idx % ATT_SLOTS)

    def key_blocks(t, carry):
        j = t * ATT_UNROLL
        items = [(j + b, si, None) for b in range(ATT_UNROLL) for si in range(n_strips)]
        items += [(j + ATT_UNROLL, si, "positions" if si == 0 else None)
                  for si in range(n_strips)]
        run(items, ATT_UNROLL * n_strips)
        return carry

    first = qi * n_strips
    diagonal = [(first + c, si, "diagonal" if si == c else None)
                for c in range(n_strips) for si in range(c, n_strips)]

    acc_ref[...] = jnp.zeros_like(acc_ref)
    m_ref[...] = jnp.full_like(m_ref, -jnp.inf)
    for idx in range(ATT_LOOKAHEAD):
        scores((0, idx, "positions" if idx == 0 else None), idx)
    lax.fori_loop(0, qi * (n_strips // ATT_UNROLL), key_blocks, 0)
    run(diagonal, len(diagonal))

    o_t = acc_ref[:V_DIM] * (1.0 / acc_ref[V_DIM:V_DIM + 1])
    o_ref[...] = (o_t.T * _silu(z_ref[...].astype(F32))).astype(BF16)


def _attn(qt, k, vt, main, batch, seq):
    assert ATT_TQ % ATT_TK == 0
    nq = seq // ATT_TQ
    z_col0 = MAIN_SECTIONS.index("z_b") * D_MODEL // V_DIM
    head = lambda b, h, i: b * MLA_HEADS + h
    return pl.pallas_call(
        _attn_body,
        grid=(batch, MLA_HEADS, nq),
        in_specs=[pl.BlockSpec((1, QK_PAD, ATT_TQ), lambda b, h, i: (head(b, h, i), 0, i)),
                  pl.BlockSpec((1, seq, QK_PAD), lambda b, h, i: (head(b, h, i), 0, 0)),
                  pl.BlockSpec((1, seq // ATT_TK, V_EXT, ATT_TK),
                               lambda b, h, i: (head(b, h, i), 0, 0, 0)),
                  pl.BlockSpec((ATT_TQ, V_DIM), lambda b, h, i: (b * nq + i, z_col0 + h))],
        out_specs=pl.BlockSpec((ATT_TQ, V_DIM), lambda b, h, i: (b * nq + i, h)),
        out_shape=jax.ShapeDtypeStruct((batch * seq, MLA_WIDTH), BF16),
        scratch_shapes=[pltpu.VMEM((ATT_SLOTS, ATT_TK, ATT_TK), F32),
                        pltpu.VMEM((ATT_SLOTS, 1, ATT_TK), F32),
                        pltpu.VMEM((V_EXT, ATT_TQ), F32), pltpu.VMEM((1, ATT_TQ), F32)],
        compiler_params=pltpu.CompilerParams(
            dimension_semantics=("arbitrary", "arbitrary", "arbitrary"),
            vmem_limit_bytes=VMEM_LIMIT),
        name="attn",
    )(qt, k, vt, main)


def _merge_body(x_ref, ya_ref, yb_ref, ga_ref, gb_ref, bg_ref, wpa_ref, wpb_ref, wout_ref,
                fg_ref, o_ref):
    gate_a = _sigmoid(ga_ref[...].astype(F32) + bg_ref[:, :D_MODEL])
    gate_b = _sigmoid(gb_ref[...].astype(F32) + bg_ref[:, D_MODEL:])
    merged = gate_a * _dot(ya_ref[...], wpa_ref[...]) + gate_b * _dot(yb_ref[...], wpb_ref[...])
    x_new = x_ref[...] + _dot(merged.astype(BF16), wout_ref[...])
    o_ref[...] = _rms(x_new, fg_ref[...])


def _merge(x2, ya, yb, main, b_gate, wpa, wpb, wout, fg):
    t = x2.shape[0]
    gates = MAIN_SECTIONS.index("gates")
    row = lambda j: pl.BlockSpec((MERGE_TM, D_MODEL), lambda i: (i, j))
    return pl.pallas_call(
        _merge_body,
        grid=(t // MERGE_TM,),
        in_specs=[row(0), row(0), row(0), row(gates), row(gates + 1), _resident(b_gate.shape),
                  _resident(wpa.shape), _resident(wpb.shape), _resident(wout.shape),
                  _resident(fg.shape)],
        out_specs=row(0),
        out_shape=jax.ShapeDtypeStruct((t, D_MODEL), F32),
        compiler_params=pltpu.CompilerParams(
            dimension_semantics=("arbitrary",), vmem_limit_bytes=VMEM_LIMIT),
        name="merge",
    )(x2, ya, yb, main, main, b_gate, wpa, wpb, wout, fg)


def _rope_tables(seq):
    inv = ROPE_THETA ** (-jnp.arange(0, QK_ROPE, 2, dtype=F32) / QK_ROPE)
    ang = jnp.arange(seq, dtype=F32)[:, None] * inv[None, :]
    cos, sin = jnp.cos(ang), jnp.sin(ang)
    zero = jnp.zeros_like(cos)
    cos_tab = jnp.concatenate([cos, cos, zero, zero], axis=-1)
    sin_tab = jnp.concatenate([-sin, sin, zero, zero], axis=-1)
    return cos_tab, sin_tab, cos_tab[:, :QK_ROPE].T, sin_tab[:, :QK_ROPE].T


def _layer(x2, batch, seq, norm_g, w_in, b_gate, lb, hg_norm_g, q_a_g, w_uq, kv_a_g, w_ukv,
           w_proj_a, w_proj_b, w_out, out_g, rope_tabs):
    half = QK_ROPE // 2
    o = 0
    w_hq, w_hf, w_hi, w_hz = (w_in[:, o + i * HG_WIDTH:o + (i + 1) * HG_WIDTH] for i in range(4))
    o += 4 * HG_WIDTH
    w_cq = w_in[:, o:o + Q_LORA]; o += Q_LORA
    w_ckv = w_in[:, o:o + KV_LORA]; o += KV_LORA
    w_kr = w_in[:, o:o + QK_ROPE]; o += QK_ROPE
    w_mz = w_in[:, o:o + MLA_WIDTH]; o += MLA_WIDTH
    w_gl = w_in[:, o:]
    kr_pad = jnp.zeros((D_MODEL, LANES - QK_ROPE), F32)
    w_kr_swapped = jnp.concatenate([w_kr[:, half:], w_kr[:, :half]], axis=1)
    w_main = jnp.concatenate([w_hq, w_hi, w_hz, w_mz, w_gl], axis=1).astype(BF16)
    w_mla = jnp.concatenate([w_cq, w_ckv, w_kr, kr_pad, w_kr_swapped, kr_pad], axis=1).astype(BF16)

    uq = w_uq.reshape(Q_LORA, MLA_HEADS, QK_DIM)
    q1, q2 = uq[:, :, QK_NOPE:QK_NOPE + half], uq[:, :, QK_NOPE + half:]
    wqa = uq.reshape(Q_LORA, MLA_HEADS * QK_DIM).T.astype(BF16)
    wqb = jnp.concatenate([q2, q1], axis=-1).reshape(Q_LORA, MLA_HEADS * QK_ROPE).T.astype(BF16)
    ukv = w_ukv.reshape(KV_LORA, MLA_HEADS, QK_NOPE + V_DIM)
    wkn = ukv[:, :, :QK_NOPE].reshape(KV_LORA, MLA_HEADS * QK_NOPE).astype(BF16)
    wv = ukv[:, :, QK_NOPE:].reshape(KV_LORA, MLA_WIDTH).T.astype(BF16)

    main, lf_hi, lf_lo, k_gate, mla = _proj(x2, norm_g[None], lb[None], w_main,
                                            w_hf.astype(BF16), w_mla)
    y_a = _hgrn(main, lf_hi, lf_lo, k_gate, hg_norm_g[None], batch, seq)
    q_scale = QK_DIM ** -0.5 * math.log2(math.e)
    qt, k, vt = _mla_up(mla, q_a_g[None], kv_a_g[None], wqa, wqb, wkn, wv, rope_tabs,
                        batch, seq, q_scale)
    y_b = _attn(qt, k, vt, main, batch, seq)
    return _merge(x2, y_a, y_b, main, b_gate[None], w_proj_a.astype(BF16),
                  w_proj_b.astype(BF16), w_out.astype(BF16), out_g[None])


def kernel(x, norm_g, w_in, b_gate, lb_logits, hg_norm_g, q_a_g, w_uq, kv_a_g, w_ukv,
           w_proj_a, w_proj_b, w_out, final_norm_g):
    batch, seq, _ = x.shape
    depth = norm_g.shape[0]
    assert depth == 1, "the final RMSNorm is fused into the single layer's merge kernel"
    lower_bounds = jnp.cumsum(jax.nn.softmax(lb_logits.astype(F32), axis=0), axis=0)[:depth]
    rope_tabs = _rope_tables(seq)
    x2 = x.reshape(batch * seq, D_MODEL)
    out = _layer(x2, batch, seq, norm_g[0], w_in[0], b_gate[0], lower_bounds[0], hg_norm_g[0],
                 q_a_g[0], w_uq[0], kv_a_g[0], w_ukv[0], w_proj_a[0], w_proj_b[0], w_out[0],
                 final_norm_g, rope_tabs)
    return out.reshape(batch, seq, D_MODEL)
```

```python
import functools
import math

import jax
import jax.numpy as jnp
from jax import lax
from jax.experimental import pallas as pl
from jax.experimental.pallas import tpu as pltpu

F32 = jnp.float32
BF16 = jnp.bfloat16

D_MODEL = 1024
HG_HEADS = 8
HG_DIM = 128
HG_WIDTH = HG_HEADS * HG_DIM
HG_CHUNK = 32
MLA_HEADS = 8
QK_NOPE = 128
QK_ROPE = 64
QK_DIM = QK_NOPE + QK_ROPE
V_DIM = 128
Q_LORA = 3 * D_MODEL // 8
KV_LORA = D_MODEL // 4
MLA_WIDTH = MLA_HEADS * V_DIM
ROPE_THETA = 10000.0
EPS = 1e-6

LANES = 128
QK_PAD = 2 * LANES
HG_GROUP = 128
VMEM_LIMIT = 56 * 1024 * 1024

PROJ_TM = 512
PROJ_CW = 512
HG_TC = 512
MLA_TS = 512
ATT_TQ = 2048
ATT_TK = 512
ATT_LOOKAHEAD = 2
ATT_SLOTS = ATT_TQ // ATT_TK
ATT_UNROLL = 4
BF16_ROWS = 16
V_EXT = V_DIM + BF16_ROWS
MERGE_TM = 512

MAIN_SECTIONS = ("q", "v", "z_a", "z_b", "gates", "gates")
MAIN_W = len(MAIN_SECTIONS) * D_MODEL
MLA_W = Q_LORA + KV_LORA + 2 * LANES


def _resident(shape):
    return pl.BlockSpec(shape, lambda *_: (0,) * len(shape), pipeline_mode=pl.Buffered(1))


def _rms(x, g):
    return x * lax.rsqrt(jnp.mean(x * x, axis=-1, keepdims=True) + EPS) * g


def _dot(a, b):
    return jnp.dot(a, b, preferred_element_type=F32)


def _dot_nt(a, b):
    return lax.dot_general(a, b, (((1,), (1,)), ((), ())), preferred_element_type=F32)


def _sigmoid(x):
    return 0.5 * jnp.tanh(0.5 * x) + 0.5


def _silu(x):
    half = 0.5 * x
    return half + half * jnp.tanh(half)


def _proj_body(x_ref, g_ref, lb_ref, wmain_ref, whf_ref, wmla_ref,
               main_ref, lfh_ref, lfl_ref, k_ref, mla_ref):
    h = _rms(x_ref[...], g_ref[...]).astype(BF16)

    def main_chunk(c):
        cs = slice(c * PROJ_CW, (c + 1) * PROJ_CW)
        y = _dot(h, wmain_ref[:, cs])
        if MAIN_SECTIONS[c * PROJ_CW // D_MODEL] in ("q", "z_a"):
            y = _silu(y)
        main_ref[:, cs] = y.astype(BF16)

    def forget_chunk(c):
        cs = slice(c * PROJ_CW, (c + 1) * PROJ_CW)
        lb = lb_ref[:, cs]
        span = 0.5 * (1.0 - lb)
        mid = 0.5 * (1.0 + lb)
        span_t = span * jnp.tanh(0.5 * _dot(h, whf_ref[:, cs]))
        f = mid + span_t
        log_f = jnp.log2(f)
        lf_hi = log_f.astype(BF16)
        lfh_ref[:, cs] = lf_hi
        lfl_ref[:, cs] = (log_f - lf_hi.astype(F32)).astype(BF16)
        k_ref[:, cs] = (span - span_t).astype(BF16)

    plain = [c for c in range(MAIN_W // PROJ_CW)
             if MAIN_SECTIONS[c * PROJ_CW // D_MODEL] not in ("q", "z_a")]
    heavy = ([("forget", c) for c in range(HG_WIDTH // PROJ_CW)]
             + [("main", c) for c in range(MAIN_W // PROJ_CW) if c not in plain])
    for kind, c in heavy:
        forget_chunk(c) if kind == "forget" else main_chunk(c)
        if plain:
            main_chunk(plain.pop(0))
    for c in plain:
        main_chunk(c)
    mla_ref[...] = _dot(h, wmla_ref[...]).astype(BF16)


def _proj(x2, norm_g, lb, w_main, w_hf, w_mla):
    t = x2.shape[0]
    row = lambda w: pl.BlockSpec((PROJ_TM, w), lambda i: (i, 0))
    out = lambda w: jax.ShapeDtypeStruct((t, w), BF16)
    return pl.pallas_call(
        _proj_body,
        grid=(t // PROJ_TM,),
        in_specs=[row(D_MODEL), _resident((1, D_MODEL)), _resident(lb.shape),
                  _resident(w_main.shape),
                  _resident(w_hf.shape), _resident(w_mla.shape)],
        out_specs=[row(MAIN_W), row(HG_WIDTH), row(HG_WIDTH), row(HG_WIDTH), row(MLA_W)],
        out_shape=[out(MAIN_W), out(HG_WIDTH), out(HG_WIDTH), out(HG_WIDTH), out(MLA_W)],
        compiler_params=pltpu.CompilerParams(
            dimension_semantics=("arbitrary",), vmem_limit_bytes=VMEM_LIMIT),
        name="proj",
    )(x2, norm_g, lb, w_main, w_hf, w_mla)


def _hgrn_body(q_ref, lfh_ref, lfl_ref, k_ref, v_ref, z_ref, g_ref, o_ref, st_ref):
    @pl.when(pl.program_id(1) == 0)
    def _():
        st_ref[...] = jnp.zeros_like(st_ref)

    n_chunks = HG_GROUP // HG_CHUNK
    shift = HG_CHUNK.bit_length() - 1
    r = lax.broadcasted_iota(jnp.int32, (HG_GROUP, HG_GROUP), 0)
    c = lax.broadcasted_iota(jnp.int32, (HG_GROUP, HG_GROUP), 1)
    col_chunk = lax.shift_right_logical(c, shift)
    causal = (lax.shift_right_logical(r, shift) == col_chunk) & (r >= c)
    prefix_mat = jnp.where(causal, 1.0, 0.0).astype(BF16)
    r_blk = lax.broadcasted_iota(jnp.int32, (HG_CHUNK, HG_GROUP), 0)
    c_blk = lax.broadcasted_iota(jnp.int32, (HG_CHUNK, HG_GROUP), 1)
    key_chunk_is = [lax.shift_right_logical(c_blk, shift) == kc for kc in range(n_chunks)]
    causal_blk = [key_chunk_is[qc] & (c_blk <= r_blk + qc * HG_CHUNK) for qc in range(n_chunks)]
    pairs = [(kc + gap, kc) for gap in range(1, n_chunks) for kc in range(n_chunks - gap)]

    def chunk_rows(ci):
        return slice(ci * HG_CHUNK, (ci + 1) * HG_CHUNK)

    def by_chunk(fn, x):
        return jnp.concatenate([fn(ci, x[chunk_rows(ci)]) for ci in range(n_chunks)], axis=0)

    def group(gi, carry):
        rows = pl.ds(pl.multiple_of(gi * HG_GROUP, HG_GROUP), HG_GROUP)
        b_all = (_dot(prefix_mat, lfh_ref[rows, :])
                 + _dot(prefix_mat, lfl_ref[rows, :]))

        partial = []
        for h in range(HG_HEADS):
            cols = slice(h * HG_DIM, (h + 1) * HG_DIM)
            b = b_all[:, cols]
            k = k_ref[rows, cols].astype(F32)
            tot = [b[(ci + 1) * HG_CHUNK - 1:(ci + 1) * HG_CHUNK] for ci in range(n_chunks)]
            zero = jnp.zeros_like(tot[0])
            before = [zero]
            for ci in range(1, n_chunks):
                before.append(before[-1] + tot[ci - 1])
            after = [zero]
            for ci in range(n_chunks - 2, -1, -1):
                after.insert(0, after[0] + tot[ci + 1])
            q_in = q_ref[rows, cols].astype(F32) * jnp.exp2(b)
            k_in = k * jnp.exp2(-b)
            k_out = by_chunk(lambda ci, x: x * jnp.exp2(tot[ci]), k_in).astype(BF16)
            q_start = by_chunk(lambda ci, x: x * jnp.exp2(before[ci]), q_in).astype(BF16)
            k_end = by_chunk(lambda ci, x: x * jnp.exp2(tot[ci] + after[ci]), k_in).astype(BF16)
            k_in = k_in.astype(BF16)
            q_cross = []
            for qc, kc in pairs:
                q_blk = q_in[chunk_rows(qc)]
                if qc > kc + 1:
                    q_blk = q_blk * jnp.exp2(before[qc] - before[kc + 1])
                q_cross.append(q_blk)
            q_cross = jnp.concatenate(q_cross, axis=0).astype(BF16)

            v = v_ref[rows, cols]
            v_t = v.T
            st = st_ref[h]
            same = _dot_nt(q_in.astype(BF16), k_in)
            cross = _dot_nt(q_cross, k_out)
            o_start = _dot_nt(q_start, st.astype(BF16))
            st_ref[h] = st * jnp.exp2(before[-1] + tot[-1]) + _dot(v_t, k_end)
            partial.append((same, cross, o_start, v))

        for h in range(HG_HEADS):
            cols = slice(h * HG_DIM, (h + 1) * HG_DIM)
            same, cross, o_start, v = partial[h]
            score_rows = []
            for qc in range(n_chunks):
                blk = jnp.where(causal_blk[qc], same[chunk_rows(qc)], 0.0)
                for idx, (pq, pk) in enumerate(pairs):
                    if pq == qc:
                        blk = jnp.where(key_chunk_is[pk], cross[chunk_rows(idx)], blk)
                score_rows.append(blk)
            scores = jnp.concatenate(score_rows, axis=0).astype(BF16)
            o = _dot(scores, v) + o_start
            y = _rms(o, g_ref[...]) * z_ref[rows, cols].astype(F32)
            o_ref[rows, cols] = y.astype(BF16)
        return carry

    lax.fori_loop(0, HG_TC // HG_GROUP, group, 0, unroll=True)


def _hgrn(main, lf_hi, lf_lo, k, hg_g, batch, seq):
    nblk = seq // HG_TC
    t = batch * seq

    def col(name=None):
        j = 0 if name is None else MAIN_SECTIONS.index(name)
        return pl.BlockSpec((HG_TC, HG_WIDTH), lambda b, s: (b * nblk + s, j))

    return pl.pallas_call(
        _hgrn_body,
        grid=(batch, nblk),
        in_specs=[col("q"), col(), col(), col(), col("v"), col("z_a"), _resident((1, HG_DIM))],
        out_specs=col(),
        out_shape=jax.ShapeDtypeStruct((t, HG_WIDTH), BF16),
        scratch_shapes=[pltpu.VMEM((HG_HEADS, HG_DIM, HG_DIM), F32)],
        compiler_params=pltpu.CompilerParams(
            dimension_semantics=("arbitrary", "arbitrary"), vmem_limit_bytes=VMEM_LIMIT),
        name="hgrn2",
    )(main, lf_hi, lf_lo, k, main, main, hg_g)


def _mla_up_body(mla_ref, qg_ref, kvg_ref, wqa_ref, wqb_ref, wkn_ref, wv_ref,
                 cos_ref, sin_ref, cos_t_ref, sin_t_ref, qt_ref, kn_ref, kpe_ref, vt_ref,
                 *, q_scale):
    lat = mla_ref[...].astype(F32)
    cq = _rms(lat[:, :Q_LORA], qg_ref[...]).astype(BF16)
    ckv = _rms(lat[:, Q_LORA:Q_LORA + KV_LORA], kvg_ref[...]).astype(BF16)
    kr_a = lat[:, Q_LORA + KV_LORA:Q_LORA + KV_LORA + LANES]
    kr_b = lat[:, Q_LORA + KV_LORA + LANES:]
    kpe_ref[...] = (kr_a * cos_ref[...] + kr_b * sin_ref[...]).astype(BF16)
    k_nope = _dot(ckv, wkn_ref[...])
    qa_t = _dot_nt(wqa_ref[...], cq)
    qb_t = _dot_nt(wqb_ref[...], cq)
    v_t = _dot_nt(wv_ref[...], ckv)
    cos_t = cos_t_ref[...]
    sin_t = sin_t_ref[...]
    ones_row = jnp.where(lax.broadcasted_iota(jnp.int32, (BF16_ROWS, ATT_TK), 0) == 0,
                         1.0, 0.0).astype(BF16)
    q_zero = jnp.zeros((QK_PAD - QK_DIM, MLA_TS), BF16)
    for h in range(MLA_HEADS):
        hs = slice(h * LANES, (h + 1) * LANES)
        q_h = qa_t[h * QK_DIM:(h + 1) * QK_DIM]
        qt_ref[h, :QK_NOPE, :] = (q_h[:QK_NOPE] * q_scale).astype(BF16)
        q_pe = q_h[QK_NOPE:] * cos_t + qb_t[h * QK_ROPE:(h + 1) * QK_ROPE] * sin_t
        qt_ref[h, QK_NOPE:QK_DIM, :] = (q_pe * q_scale).astype(BF16)
        qt_ref[h, QK_DIM:, :] = q_zero
        kn_ref[h] = k_nope[:, hs].astype(BF16)
        for c in range(MLA_TS // ATT_TK):
            vt_ref[h, c, :V_DIM, :] = v_t[hs, c * ATT_TK:(c + 1) * ATT_TK].astype(BF16)
            vt_ref[h, c, V_DIM:, :] = ones_row


def _mla_up(mla, qg, kvg, wqa, wqb, wkn, wv, tabs, batch, seq, q_scale):
    nblk = seq // MLA_TS
    kv_per_step = MLA_TS // ATT_TK
    tab = pl.BlockSpec((MLA_TS, LANES), lambda b, s: (s, 0))
    tab_t = pl.BlockSpec((QK_ROPE, MLA_TS), lambda b, s: (0, s))
    return pl.pallas_call(
        functools.partial(_mla_up_body, q_scale=q_scale),
        grid=(batch, nblk),
        in_specs=[pl.BlockSpec((MLA_TS, MLA_W), lambda b, s: (b * nblk + s, 0)),
                  _resident(qg.shape), _resident(kvg.shape), _resident(wqa.shape),
                  _resident(wqb.shape), _resident(wkn.shape), _resident(wv.shape),
                  tab, tab, tab_t, tab_t],
        out_specs=[pl.BlockSpec((MLA_HEADS, QK_PAD, MLA_TS), lambda b, s: (b, 0, s)),
                   pl.BlockSpec((MLA_HEADS, MLA_TS, QK_NOPE), lambda b, s: (b, s, 0)),
                   pl.BlockSpec((MLA_TS, LANES), lambda b, s: (b * nblk + s, 0)),
                   pl.BlockSpec((MLA_HEADS, kv_per_step, V_EXT, ATT_TK),
                                lambda b, s: (b, s, 0, 0))],
        out_shape=[jax.ShapeDtypeStruct((batch * MLA_HEADS, QK_PAD, seq), BF16),
                   jax.ShapeDtypeStruct((batch * MLA_HEADS, seq, QK_NOPE), BF16),
                   jax.ShapeDtypeStruct((batch * seq, LANES), BF16),
                   jax.ShapeDtypeStruct((batch * MLA_HEADS, seq // ATT_TK, V_EXT, ATT_TK), BF16)],
        compiler_params=pltpu.CompilerParams(
            dimension_semantics=("arbitrary", "arbitrary"), vmem_limit_bytes=VMEM_LIMIT),
        name="mla_up",
    )(mla, qg, kvg, wqa, wqb, wkn, wv, *tabs)


def _attn_body(qt_ref, kn_ref, kpe_ref, vt_ref, z_ref, o_ref, s_ref, smax_ref, acc_ref, m_ref):
    qi = pl.program_id(2)
    n_strips = ATT_TQ // ATT_TK

    def lanes_of(si):
        return slice(si * ATT_TK, (si + 1) * ATT_TK)

    def scores(item, slot):
        j, si, mask = item
        ks = pl.ds(pl.multiple_of(j * ATT_TK, ATT_TK), ATT_TK)
        keys = jnp.concatenate([kn_ref[0, ks, :], kpe_ref[ks, :]], axis=1)
        s = _dot(keys, qt_ref[0, :, lanes_of(si)])
        if mask is not None:
            key = lax.broadcasted_iota(jnp.int32, s.shape, 0)
            query = lax.broadcasted_iota(jnp.int32, s.shape, 1)
            if mask == "positions":
                query = query + ((qi * n_strips + si - j) * ATT_TK)
            s = jnp.where(query >= key, s, -jnp.inf)
        s_ref[slot] = s
        smax_ref[slot] = jnp.max(s, axis=0, keepdims=True)

    def softmax_pv(item, slot):
        j, si, _ = item
        lanes = lanes_of(si)
        m_prev = m_ref[:, lanes]
        m_new = jnp.maximum(m_prev, smax_ref[slot])
        alpha = jnp.exp2(m_prev - m_new)
        p = jnp.exp2(s_ref[slot] - m_new).astype(BF16)
        acc_ref[:, lanes] = alpha * acc_ref[:, lanes] + _dot(vt_ref[0, j], p)
        m_ref[:, lanes] = m_new

    def run(items, n_consume):
        for idx in range(n_consume):
            ahead = idx + ATT_LOOKAHEAD
            if ahead < len(items):
                scores(items[ahead], ahead % ATT_SLOTS)
            softmax_pv(items[idx], idx % ATT_SLOTS)
            if items[idx][2] == "diagonal":
                finish_strip(items[idx][1])

    def finish_strip(si):
        lanes = lanes_of(si)
        o_t = acc_ref[:V_DIM, lanes] * (1.0 / acc_ref[V_DIM:V_DIM + 1, lanes])
        o_ref[lanes, :] = (o_t.T * _silu(z_ref[lanes, :].astype(F32))).astype(BF16)

    def key_blocks(t, carry):
        j = t * ATT_UNROLL
        items = [(j + b, si, None) for b in range(ATT_UNROLL) for si in range(n_strips)]
        items += [(j + ATT_UNROLL, si, "positions" if si == 0 else None)
                  for si in range(n_strips)]
        run(items, ATT_UNROLL * n_strips)
        return carry

    first = qi * n_strips
    diagonal = [(first + c, si, "diagonal" if si == c else None)
                for c in range(n_strips) for si in range(c, n_strips)]

    acc_ref[...] = jnp.zeros_like(acc_ref)
    m_ref[...] = jnp.full_like(m_ref, -jnp.inf)
    for idx in range(ATT_LOOKAHEAD):
        scores((0, idx, "positions" if idx == 0 else None), idx)
    lax.fori_loop(0, qi * (n_strips // ATT_UNROLL), key_blocks, 0)
    run(diagonal, len(diagonal))


def _attn(qt, k_nope, k_pe, vt, main, batch, seq):
    assert ATT_TQ % ATT_TK == 0
    nq = seq // ATT_TQ
    z_col0 = MAIN_SECTIONS.index("z_b") * D_MODEL // V_DIM
    head = lambda b, h, i: b * MLA_HEADS + h
    return pl.pallas_call(
        _attn_body,
        grid=(batch, MLA_HEADS, nq),
        in_specs=[pl.BlockSpec((1, QK_PAD, ATT_TQ), lambda b, h, i: (head(b, h, i), 0, i)),
                  pl.BlockSpec((1, seq, QK_NOPE), lambda b, h, i: (head(b, h, i), 0, 0)),
                  pl.BlockSpec((seq, LANES), lambda b, h, i: (b, 0)),
                  pl.BlockSpec((1, seq // ATT_TK, V_EXT, ATT_TK),
                               lambda b, h, i: (head(b, h, i), 0, 0, 0)),
                  pl.BlockSpec((ATT_TQ, V_DIM), lambda b, h, i: (b * nq + i, z_col0 + h))],
        out_specs=pl.BlockSpec((ATT_TQ, V_DIM), lambda b, h, i: (b * nq + i, h)),
        out_shape=jax.ShapeDtypeStruct((batch * seq, MLA_WIDTH), BF16),
        scratch_shapes=[pltpu.VMEM((ATT_SLOTS, ATT_TK, ATT_TK), F32),
                        pltpu.VMEM((ATT_SLOTS, 1, ATT_TK), F32),
                        pltpu.VMEM((V_EXT, ATT_TQ), F32), pltpu.VMEM((1, ATT_TQ), F32)],
        compiler_params=pltpu.CompilerParams(
            dimension_semantics=("arbitrary", "arbitrary", "arbitrary"),
            vmem_limit_bytes=VMEM_LIMIT),
        name="attn",
    )(qt, k_nope, k_pe, vt, main)


def _merge_body(x_ref, ya_ref, yb_ref, ga_ref, gb_ref, bg_ref, wpa_ref, wpb_ref, wout_ref,
                fg_ref, o_ref):
    gate_a = _sigmoid(ga_ref[...].astype(F32) + bg_ref[:, :D_MODEL])
    gate_b = _sigmoid(gb_ref[...].astype(F32) + bg_ref[:, D_MODEL:])
    merged = gate_a * _dot(ya_ref[...], wpa_ref[...]) + gate_b * _dot(yb_ref[...], wpb_ref[...])
    x_new = x_ref[...] + _dot(merged.astype(BF16), wout_ref[...])
    o_ref[...] = _rms(x_new, fg_ref[...])


def _merge(x2, ya, yb, main, b_gate, wpa, wpb, wout, fg):
    t = x2.shape[0]
    gates = MAIN_SECTIONS.index("gates")
    row = lambda j: pl.BlockSpec((MERGE_TM, D_MODEL), lambda i: (i, j))
    return pl.pallas_call(
        _merge_body,
        grid=(t // MERGE_TM,),
        in_specs=[row(0), row(0), row(0), row(gates), row(gates + 1), _resident(b_gate.shape),
                  _resident(wpa.shape), _resident(wpb.shape), _resident(wout.shape),
                  _resident(fg.shape)],
        out_specs=row(0),
        out_shape=jax.ShapeDtypeStruct((t, D_MODEL), F32),
        compiler_params=pltpu.CompilerParams(
            dimension_semantics=("arbitrary",), vmem_limit_bytes=VMEM_LIMIT),
        name="merge",
    )(x2, ya, yb, main, main, b_gate, wpa, wpb, wout, fg)


def _rope_tables(seq):
    inv = ROPE_THETA ** (-jnp.arange(0, QK_ROPE, 2, dtype=F32) / QK_ROPE)
    ang = jnp.arange(seq, dtype=F32)[:, None] * inv[None, :]
    cos, sin = jnp.cos(ang), jnp.sin(ang)
    zero = jnp.zeros_like(cos)
    cos_tab = jnp.concatenate([cos, cos, zero, zero], axis=-1)
    sin_tab = jnp.concatenate([-sin, sin, zero, zero], axis=-1)
    return cos_tab, sin_tab, cos_tab[:, :QK_ROPE].T, sin_tab[:, :QK_ROPE].T


def _layer(x2, batch, seq, norm_g, w_in, b_gate, lb, hg_norm_g, q_a_g, w_uq, kv_a_g, w_ukv,
           w_proj_a, w_proj_b, w_out, out_g, rope_tabs):
    half = QK_ROPE // 2
    o = 0
    w_hq, w_hf, w_hi, w_hz = (w_in[:, o + i * HG_WIDTH:o + (i + 1) * HG_WIDTH] for i in range(4))
    o += 4 * HG_WIDTH
    w_cq = w_in[:, o:o + Q_LORA]; o += Q_LORA
    w_ckv = w_in[:, o:o + KV_LORA]; o += KV_LORA
    w_kr = w_in[:, o:o + QK_ROPE]; o += QK_ROPE
    w_mz = w_in[:, o:o + MLA_WIDTH]; o += MLA_WIDTH
    w_gl = w_in[:, o:]
    kr_pad = jnp.zeros((D_MODEL, LANES - QK_ROPE), F32)
    w_kr_swapped = jnp.concatenate([w_kr[:, half:], w_kr[:, :half]], axis=1)
    w_main = jnp.concatenate([w_hq, w_hi, w_hz, w_mz, w_gl], axis=1).astype(BF16)
    w_mla = jnp.concatenate([w_cq, w_ckv, w_kr, kr_pad, w_kr_swapped, kr_pad], axis=1).astype(BF16)

    uq = w_uq.reshape(Q_LORA, MLA_HEADS, QK_DIM)
    q1, q2 = uq[:, :, QK_NOPE:QK_NOPE + half], uq[:, :, QK_NOPE + half:]
    wqa = uq.reshape(Q_LORA, MLA_HEADS * QK_DIM).T.astype(BF16)
    wqb = jnp.concatenate([q2, q1], axis=-1).reshape(Q_LORA, MLA_HEADS * QK_ROPE).T.astype(BF16)
    ukv = w_ukv.reshape(KV_LORA, MLA_HEADS, QK_NOPE + V_DIM)
    wkn = ukv[:, :, :QK_NOPE].reshape(KV_LORA, MLA_HEADS * QK_NOPE).astype(BF16)
    wv = ukv[:, :, QK_NOPE:].reshape(KV_LORA, MLA_WIDTH).T.astype(BF16)

    main, lf_hi, lf_lo, k_gate, mla = _proj(x2, norm_g[None], lb[None], w_main,
                                            w_hf.astype(BF16), w_mla)
    y_a = _hgrn(main, lf_hi, lf_lo, k_gate, hg_norm_g[None], batch, seq)
    q_scale = QK_DIM ** -0.5 * math.log2(math.e)
    qt, k_nope, k_pe, vt = _mla_up(mla, q_a_g[None], kv_a_g[None], wqa, wqb, wkn, wv, rope_tabs,
                                   batch, seq, q_scale)
    y_b = _attn(qt, k_nope, k_pe, vt, main, batch, seq)
    return _merge(x2, y_a, y_b, main, b_gate[None], w_proj_a.astype(BF16),
                  w_proj_b.astype(BF16), w_out.astype(BF16), out_g[None])


def kernel(x, norm_g, w_in, b_gate, lb_logits, hg_norm_g, q_a_g, w_uq, kv_a_g, w_ukv,
           w_proj_a, w_proj_b, w_out, final_norm_g):
    batch, seq, _ = x.shape
    depth = norm_g.shape[0]
    assert depth == 1, "the final RMSNorm is fused into the single layer's merge kernel"
    lower_bounds = jnp.cumsum(jax.nn.softmax(lb_logits.astype(F32), axis=0), axis=0)[:depth]
    rope_tabs = _rope_tables(seq)
    x2 = x.reshape(batch * seq, D_MODEL)
    out = _layer(x2, batch, seq, norm_g[0], w_in[0], b_gate[0], lower_bounds[0], hg_norm_g[0],
                 q_a_g[0], w_uq[0], kv_a_g[0], w_ukv[0], w_proj_a[0], w_proj_b[0], w_out[0],
                 final_norm_g, rope_tabs)
    return out.reshape(batch, seq, D_MODEL)
```

```python
import functools
import math

import jax
import jax.numpy as jnp
from jax import lax
from jax.experimental import pallas as pl
from jax.experimental.pallas import tpu as pltpu

F32 = jnp.float32
BF16 = jnp.bfloat16

D_MODEL = 1024
HG_HEADS = 8
HG_DIM = 128
HG_WIDTH = HG_HEADS * HG_DIM
HG_CHUNK = 32
MLA_HEADS = 8
QK_NOPE = 128
QK_ROPE = 64
QK_DIM = QK_NOPE + QK_ROPE
V_DIM = 128
Q_LORA = 3 * D_MODEL // 8
KV_LORA = D_MODEL // 4
MLA_WIDTH = MLA_HEADS * V_DIM
ROPE_THETA = 10000.0
EPS = 1e-6

LANES = 128
QK_PAD = 2 * LANES
HG_GROUP = 128
VMEM_LIMIT = 56 * 1024 * 1024

PROJ_TM = 512
PROJ_CW = 512
HG_TC = 512
MLA_TS = 512
ATT_TQ = 2048
ATT_TK = 512
ATT_LOOKAHEAD = 2
ATT_SLOTS = ATT_TQ // ATT_TK
ATT_UNROLL = 4
BF16_ROWS = 16
V_EXT = V_DIM + BF16_ROWS
MERGE_TM = 1024
MERGE_SUB = 512

MAIN_SECTIONS = ("q", "v", "z_a", "z_b", "gates", "gates")
MAIN_W = len(MAIN_SECTIONS) * D_MODEL
MLA_W = Q_LORA + KV_LORA + 2 * LANES


def _resident(shape):
    return pl.BlockSpec(shape, lambda *_: (0,) * len(shape), pipeline_mode=pl.Buffered(1))


def _rms(x, g):
    return x * lax.rsqrt(jnp.mean(x * x, axis=-1, keepdims=True) + EPS) * g


def _dot(a, b):
    return jnp.dot(a, b, preferred_element_type=F32)


def _dot_nt(a, b):
    return lax.dot_general(a, b, (((1,), (1,)), ((), ())), preferred_element_type=F32)


def _sigmoid(x):
    return 0.5 * jnp.tanh(0.5 * x) + 0.5


def _silu(x):
    half = 0.5 * x
    return half + half * jnp.tanh(half)


def _proj_body(x_ref, g_ref, lb_ref, wmain_ref, whf_ref, wmla_ref,
               main_ref, lfh_ref, lfl_ref, k_ref, mla_ref):
    h = _rms(x_ref[...], g_ref[...]).astype(BF16)

    def main_chunk(c):
        cs = slice(c * PROJ_CW, (c + 1) * PROJ_CW)
        y = _dot(h, wmain_ref[:, cs])
        if MAIN_SECTIONS[c * PROJ_CW // D_MODEL] in ("q", "z_a"):
            y = _silu(y)
        main_ref[:, cs] = y.astype(BF16)

    def forget_chunk(c):
        cs = slice(c * PROJ_CW, (c + 1) * PROJ_CW)
        lb = lb_ref[:, cs]
        span = 0.5 * (1.0 - lb)
        mid = 0.5 * (1.0 + lb)
        span_t = span * jnp.tanh(0.5 * _dot(h, whf_ref[:, cs]))
        f = mid + span_t
        log_f = jnp.log2(f)
        lf_hi = log_f.astype(BF16)
        lfh_ref[:, cs] = lf_hi
        lfl_ref[:, cs] = (log_f - lf_hi.astype(F32)).astype(BF16)
        k_ref[:, cs] = (span - span_t).astype(BF16)

    plain = [c for c in range(MAIN_W // PROJ_CW)
             if MAIN_SECTIONS[c * PROJ_CW // D_MODEL] not in ("q", "z_a")]
    heavy = ([("forget", c) for c in range(HG_WIDTH // PROJ_CW)]
             + [("main", c) for c in range(MAIN_W // PROJ_CW) if c not in plain])
    for kind, c in heavy:
        forget_chunk(c) if kind == "forget" else main_chunk(c)
        if plain:
            main_chunk(plain.pop(0))
    for c in plain:
        main_chunk(c)
    mla_ref[...] = _dot(h, wmla_ref[...]).astype(BF16)


def _proj(x2, norm_g, lb, w_main, w_hf, w_mla):
    t = x2.shape[0]
    row = lambda w: pl.BlockSpec((PROJ_TM, w), lambda i: (i, 0))
    out = lambda w: jax.ShapeDtypeStruct((t, w), BF16)
    return pl.pallas_call(
        _proj_body,
        grid=(t // PROJ_TM,),
        in_specs=[row(D_MODEL), _resident((1, D_MODEL)), _resident(lb.shape),
                  _resident(w_main.shape),
                  _resident(w_hf.shape), _resident(w_mla.shape)],
        out_specs=[row(MAIN_W), row(HG_WIDTH), row(HG_WIDTH), row(HG_WIDTH), row(MLA_W)],
        out_shape=[out(MAIN_W), out(HG_WIDTH), out(HG_WIDTH), out(HG_WIDTH), out(MLA_W)],
        compiler_params=pltpu.CompilerParams(
            dimension_semantics=("arbitrary",), vmem_limit_bytes=VMEM_LIMIT),
        name="proj",
    )(x2, norm_g, lb, w_main, w_hf, w_mla)


def _hgrn_body(q_ref, lfh_ref, lfl_ref, k_ref, v_ref, z_ref, g_ref, o_ref, st_ref):
    @pl.when(pl.program_id(1) == 0)
    def _():
        st_ref[...] = jnp.zeros_like(st_ref)

    n_chunks = HG_GROUP // HG_CHUNK
    shift = HG_CHUNK.bit_length() - 1
    r = lax.broadcasted_iota(jnp.int32, (HG_GROUP, HG_GROUP), 0)
    c = lax.broadcasted_iota(jnp.int32, (HG_GROUP, HG_GROUP), 1)
    col_chunk = lax.shift_right_logical(c, shift)
    causal = (lax.shift_right_logical(r, shift) == col_chunk) & (r >= c)
    prefix_mat = jnp.where(causal, 1.0, 0.0).astype(BF16)
    r_blk = lax.broadcasted_iota(jnp.int32, (HG_CHUNK, HG_GROUP), 0)
    c_blk = lax.broadcasted_iota(jnp.int32, (HG_CHUNK, HG_GROUP), 1)
    key_chunk_is = [lax.shift_right_logical(c_blk, shift) == kc for kc in range(n_chunks)]
    causal_blk = [key_chunk_is[qc] & (c_blk <= r_blk + qc * HG_CHUNK) for qc in range(n_chunks)]
    pairs = [(kc + gap, kc) for gap in range(1, n_chunks) for kc in range(n_chunks - gap)]

    def chunk_rows(ci):
        return slice(ci * HG_CHUNK, (ci + 1) * HG_CHUNK)

    def by_chunk(fn, x):
        return jnp.concatenate([fn(ci, x[chunk_rows(ci)]) for ci in range(n_chunks)], axis=0)

    def group(gi, carry):
        rows = pl.ds(pl.multiple_of(gi * HG_GROUP, HG_GROUP), HG_GROUP)
        b_all = (_dot(prefix_mat, lfh_ref[rows, :])
                 + _dot(prefix_mat, lfl_ref[rows, :]))

        partial = []
        for h in range(HG_HEADS):
            cols = slice(h * HG_DIM, (h + 1) * HG_DIM)
            b = b_all[:, cols]
            k = k_ref[rows, cols].astype(F32)
            tot = [b[(ci + 1) * HG_CHUNK - 1:(ci + 1) * HG_CHUNK] for ci in range(n_chunks)]
            zero = jnp.zeros_like(tot[0])
            before = [zero]
            for ci in range(1, n_chunks):
                before.append(before[-1] + tot[ci - 1])
            after = [zero]
            for ci in range(n_chunks - 2, -1, -1):
                after.insert(0, after[0] + tot[ci + 1])
            q_in = q_ref[rows, cols].astype(F32) * jnp.exp2(b)
            k_in = k * jnp.exp2(-b)
            k_out = by_chunk(lambda ci, x: x * jnp.exp2(tot[ci]), k_in).astype(BF16)
            q_start = by_chunk(lambda ci, x: x * jnp.exp2(before[ci]), q_in).astype(BF16)
            k_end = by_chunk(lambda ci, x: x * jnp.exp2(tot[ci] + after[ci]), k_in).astype(BF16)
            k_in = k_in.astype(BF16)
            q_cross = []
            for qc, kc in pairs:
                q_blk = q_in[chunk_rows(qc)]
                if qc > kc + 1:
                    q_blk = q_blk * jnp.exp2(before[qc] - before[kc + 1])
                q_cross.append(q_blk)
            q_cross = jnp.concatenate(q_cross, axis=0).astype(BF16)

            v = v_ref[rows, cols]
            v_t = v.T
            st = st_ref[h]
            same = _dot_nt(q_in.astype(BF16), k_in)
            cross = _dot_nt(q_cross, k_out)
            o_start = _dot_nt(q_start, st.astype(BF16))
            st_ref[h] = st * jnp.exp2(before[-1] + tot[-1]) + _dot(v_t, k_end)
            partial.append((same, cross, o_start, v))

        for h in range(HG_HEADS):
            cols = slice(h * HG_DIM, (h + 1) * HG_DIM)
            same, cross, o_start, v = partial[h]
            score_rows = []
            for qc in range(n_chunks):
                blk = jnp.where(causal_blk[qc], same[chunk_rows(qc)], 0.0)
                for idx, (pq, pk) in enumerate(pairs):
                    if pq == qc:
                        blk = jnp.where(key_chunk_is[pk], cross[chunk_rows(idx)], blk)
                score_rows.append(blk)
            scores = jnp.concatenate(score_rows, axis=0).astype(BF16)
            o = _dot(scores, v) + o_start
            y = _rms(o, g_ref[...]) * z_ref[rows, cols].astype(F32)
            o_ref[rows, cols] = y.astype(BF16)
        return carry

    lax.fori_loop(0, HG_TC // HG_GROUP, group, 0, unroll=True)


def _hgrn(main, lf_hi, lf_lo, k, hg_g, batch, seq):
    nblk = seq // HG_TC
    t = batch * seq

    def col(name=None):
        j = 0 if name is None else MAIN_SECTIONS.index(name)
        return pl.BlockSpec((HG_TC, HG_WIDTH), lambda b, s: (b * nblk + s, j))

    return pl.pallas_call(
        _hgrn_body,
        grid=(batch, nblk),
        in_specs=[col("q"), col(), col(), col(), col("v"), col("z_a"), _resident((1, HG_DIM))],
        out_specs=col(),
        out_shape=jax.ShapeDtypeStruct((t, HG_WIDTH), BF16),
        scratch_shapes=[pltpu.VMEM((HG_HEADS, HG_DIM, HG_DIM), F32)],
        compiler_params=pltpu.CompilerParams(
            dimension_semantics=("arbitrary", "arbitrary"), vmem_limit_bytes=VMEM_LIMIT),
        name="hgrn2",
    )(main, lf_hi, lf_lo, k, main, main, hg_g)


def _mla_up_body(mla_ref, qg_ref, kvg_ref, wqa_ref, wqb_ref, wkn_ref, wv_ref,
                 cos_ref, sin_ref, cos_t_ref, sin_t_ref, qt_ref, kn_ref, kpe_ref, vt_ref,
                 *, q_scale):
    lat = mla_ref[...].astype(F32)
    cq = _rms(lat[:, :Q_LORA], qg_ref[...]).astype(BF16)
    ckv = _rms(lat[:, Q_LORA:Q_LORA + KV_LORA], kvg_ref[...]).astype(BF16)
    kr_a = lat[:, Q_LORA + KV_LORA:Q_LORA + KV_LORA + LANES]
    kr_b = lat[:, Q_LORA + KV_LORA + LANES:]
    kpe_ref[...] = (kr_a * cos_ref[...] + kr_b * sin_ref[...]).astype(BF16)
    k_nope = _dot(ckv, wkn_ref[...])
    qa_t = _dot_nt(wqa_ref[...], cq)
    qb_t = _dot_nt(wqb_ref[...], cq)
    v_t = _dot_nt(wv_ref[...], ckv)
    cos_t = cos_t_ref[...]
    sin_t = sin_t_ref[...]
    ones_row = jnp.where(lax.broadcasted_iota(jnp.int32, (BF16_ROWS, ATT_TK), 0) == 0,
                         1.0, 0.0).astype(BF16)
    q_zero = jnp.zeros((QK_PAD - QK_DIM, MLA_TS), BF16)
    for h in range(MLA_HEADS):
        hs = slice(h * LANES, (h + 1) * LANES)
        q_h = qa_t[h * QK_DIM:(h + 1) * QK_DIM]
        qt_ref[h, :QK_NOPE, :] = (q_h[:QK_NOPE] * q_scale).astype(BF16)
        q_pe = q_h[QK_NOPE:] * cos_t + qb_t[h * QK_ROPE:(h + 1) * QK_ROPE] * sin_t
        qt_ref[h, QK_NOPE:QK_DIM, :] = (q_pe * q_scale).astype(BF16)
        qt_ref[h, QK_DIM:, :] = q_zero
        kn_ref[h] = k_nope[:, hs].astype(BF16)
        for c in range(MLA_TS // ATT_TK):
            vt_ref[h, c, :V_DIM, :] = v_t[hs, c * ATT_TK:(c + 1) * ATT_TK].astype(BF16)
            vt_ref[h, c, V_DIM:, :] = ones_row


def _mla_up(mla, qg, kvg, wqa, wqb, wkn, wv, tabs, batch, seq, q_scale):
    nblk = seq // MLA_TS
    kv_per_step = MLA_TS // ATT_TK
    tab = pl.BlockSpec((MLA_TS, LANES), lambda b, s: (s, 0))
    tab_t = pl.BlockSpec((QK_ROPE, MLA_TS), lambda b, s: (0, s))
    return pl.pallas_call(
        functools.partial(_mla_up_body, q_scale=q_scale),
        grid=(batch, nblk),
        in_specs=[pl.BlockSpec((MLA_TS, MLA_W), lambda b, s: (b * nblk + s, 0)),
                  _resident(qg.shape), _resident(kvg.shape), _resident(wqa.shape),
                  _resident(wqb.shape), _resident(wkn.shape), _resident(wv.shape),
                  tab, tab, tab_t, tab_t],
        out_specs=[pl.BlockSpec((MLA_HEADS, QK_PAD, MLA_TS), lambda b, s: (b, 0, s)),
                   pl.BlockSpec((MLA_HEADS, MLA_TS, QK_NOPE), lambda b, s: (b, s, 0)),
                   pl.BlockSpec((MLA_TS, LANES), lambda b, s: (b * nblk + s, 0)),
                   pl.BlockSpec((MLA_HEADS, kv_per_step, V_EXT, ATT_TK),
                                lambda b, s: (b, s, 0, 0))],
        out_shape=[jax.ShapeDtypeStruct((batch * MLA_HEADS, QK_PAD, seq), BF16),
                   jax.ShapeDtypeStruct((batch * MLA_HEADS, seq, QK_NOPE), BF16),
                   jax.ShapeDtypeStruct((batch * seq, LANES), BF16),
                   jax.ShapeDtypeStruct((batch * MLA_HEADS, seq // ATT_TK, V_EXT, ATT_TK), BF16)],
        compiler_params=pltpu.CompilerParams(
            dimension_semantics=("arbitrary", "arbitrary"), vmem_limit_bytes=VMEM_LIMIT),
        name="mla_up",
    )(mla, qg, kvg, wqa, wqb, wkn, wv, *tabs)


def _attn_body(qt_ref, kn_ref, kpe_ref, vt_ref, z_ref, o_ref, s_ref, smax_ref, acc_ref, m_ref):
    qi = pl.program_id(2)
    n_strips = ATT_TQ // ATT_TK

    def lanes_of(si):
        return slice(si * ATT_TK, (si + 1) * ATT_TK)

    def scores(item, slot):
        j, si, mask = item
        ks = pl.ds(pl.multiple_of(j * ATT_TK, ATT_TK), ATT_TK)
        keys = jnp.concatenate([kn_ref[0, ks, :], kpe_ref[ks, :]], axis=1)
        s = _dot(keys, qt_ref[0, :, lanes_of(si)])
        if mask is not None:
            key = lax.broadcasted_iota(jnp.int32, s.shape, 0)
            query = lax.broadcasted_iota(jnp.int32, s.shape, 1)
            if mask == "positions":
                query = query + ((qi * n_strips + si - j) * ATT_TK)
            s = jnp.where(query >= key, s, -jnp.inf)
        s_ref[slot] = s
        smax_ref[slot] = jnp.max(s, axis=0, keepdims=True)

    def softmax_pv(item, slot):
        j, si, _ = item
        lanes = lanes_of(si)
        m_prev = m_ref[:, lanes]
        m_new = jnp.maximum(m_prev, smax_ref[slot])
        alpha = jnp.exp2(m_prev - m_new)
        p = jnp.exp2(s_ref[slot] - m_new).astype(BF16)
        acc_ref[:, lanes] = alpha * acc_ref[:, lanes] + _dot(vt_ref[0, j], p)
        m_ref[:, lanes] = m_new

    def run(items, n_consume):
        for idx in range(n_consume):
            ahead = idx + ATT_LOOKAHEAD
            if ahead < len(items):
                scores(items[ahead], ahead % ATT_SLOTS)
            softmax_pv(items[idx], idx % ATT_SLOTS)
            if items[idx][2] == "diagonal":
                finish_strip(items[idx][1])

    def finish_strip(si):
        lanes = lanes_of(si)
        o_t = acc_ref[:V_DIM, lanes] * (1.0 / acc_ref[V_DIM:V_DIM + 1, lanes])
        o_ref[lanes, :] = (o_t.T * _silu(z_ref[lanes, :].astype(F32))).astype(BF16)

    def key_blocks(t, carry):
        j = t * ATT_UNROLL
        items = [(j + b, si, None) for b in range(ATT_UNROLL) for si in range(n_strips)]
        items += [(j + ATT_UNROLL, si, "positions" if si == 0 else None)
                  for si in range(n_strips)]
        run(items, ATT_UNROLL * n_strips)
        return carry

    first = qi * n_strips
    diagonal = [(first + c, si, "diagonal" if si == c else None)
                for c in range(n_strips) for si in range(c, n_strips)]

    acc_ref[...] = jnp.zeros_like(acc_ref)
    m_ref[...] = jnp.full_like(m_ref, -jnp.inf)
    for idx in range(ATT_LOOKAHEAD):
        scores((0, idx, "positions" if idx == 0 else None), idx)
    lax.fori_loop(0, qi * (n_strips // ATT_UNROLL), key_blocks, 0)
    run(diagonal, len(diagonal))


def _attn(qt, k_nope, k_pe, vt, main, batch, seq):
    assert ATT_TQ % ATT_TK == 0
    nq = seq // ATT_TQ
    z_col0 = MAIN_SECTIONS.index("z_b") * D_MODEL // V_DIM
    head = lambda b, h, i: b * MLA_HEADS + h
    return pl.pallas_call(
        _attn_body,
        grid=(batch, MLA_HEADS, nq),
        in_specs=[pl.BlockSpec((1, QK_PAD, ATT_TQ), lambda b, h, i: (head(b, h, i), 0, i)),
                  pl.BlockSpec((1, seq, QK_NOPE), lambda b, h, i: (head(b, h, i), 0, 0)),
                  pl.BlockSpec((seq, LANES), lambda b, h, i: (b, 0)),
                  pl.BlockSpec((1, seq // ATT_TK, V_EXT, ATT_TK),
                               lambda b, h, i: (head(b, h, i), 0, 0, 0)),
                  pl.BlockSpec((ATT_TQ, V_DIM), lambda b, h, i: (b * nq + i, z_col0 + h))],
        out_specs=pl.BlockSpec((ATT_TQ, V_DIM), lambda b, h, i: (b * nq + i, h)),
        out_shape=jax.ShapeDtypeStruct((batch * seq, MLA_WIDTH), BF16),
        scratch_shapes=[pltpu.VMEM((ATT_SLOTS, ATT_TK, ATT_TK), F32),
                        pltpu.VMEM((ATT_SLOTS, 1, ATT_TK), F32),
                        pltpu.VMEM((V_EXT, ATT_TQ), F32), pltpu.VMEM((1, ATT_TQ), F32)],
        compiler_params=pltpu.CompilerParams(
            dimension_semantics=("arbitrary", "arbitrary", "arbitrary"),
            vmem_limit_bytes=VMEM_LIMIT),
        name="attn",
    )(qt, k_nope, k_pe, vt, main)


def _merge_body(x_ref, ya_ref, yb_ref, ga_ref, gb_ref, bg_ref, wpa_ref, wpb_ref, wout_ref,
                fg_ref, o_ref):
    subs = [slice(r, r + MERGE_SUB) for r in range(0, MERGE_TM, MERGE_SUB)]
    branches = [(_dot(ya_ref[rs, :], wpa_ref[...]), _dot(yb_ref[rs, :], wpb_ref[...]))
                for rs in subs]
    for rs, (pa, pb) in zip(subs, branches):
        gate_a = _sigmoid(ga_ref[rs, :].astype(F32) + bg_ref[:, :D_MODEL])
        gate_b = _sigmoid(gb_ref[rs, :].astype(F32) + bg_ref[:, D_MODEL:])
        merged = (gate_a * pa + gate_b * pb).astype(BF16)
        x_new = x_ref[rs, :] + _dot(merged, wout_ref[...])
        o_ref[rs, :] = _rms(x_new, fg_ref[...])


def _merge(x2, ya, yb, main, b_gate, wpa, wpb, wout, fg):
    t = x2.shape[0]
    gates = MAIN_SECTIONS.index("gates")
    row = lambda j: pl.BlockSpec((MERGE_TM, D_MODEL), lambda i: (i, j))
    return pl.pallas_call(
        _merge_body,
        grid=(t // MERGE_TM,),
        in_specs=[row(0), row(0), row(0), row(gates), row(gates + 1), _resident(b_gate.shape),
                  _resident(wpa.shape), _resident(wpb.shape), _resident(wout.shape),
                  _resident(fg.shape)],
        out_specs=row(0),
        out_shape=jax.ShapeDtypeStruct((t, D_MODEL), F32),
        compiler_params=pltpu.CompilerParams(
            dimension_semantics=("arbitrary",), vmem_limit_bytes=VMEM_LIMIT),
        name="merge",
    )(x2, ya, yb, main, main, b_gate, wpa, wpb, wout, fg)


def _rope_tables(seq):
    inv = ROPE_THETA ** (-jnp.arange(0, QK_ROPE, 2, dtype=F32) / QK_ROPE)
    ang = jnp.arange(seq, dtype=F32)[:, None] * inv[None, :]
    cos, sin = jnp.cos(ang), jnp.sin(ang)
    zero = jnp.zeros_like(cos)
    cos_tab = jnp.concatenate([cos, cos, zero, zero], axis=-1)
    sin_tab = jnp.concatenate([-sin, sin, zero, zero], axis=-1)
    return cos_tab, sin_tab, cos_tab[:, :QK_ROPE].T, sin_tab[:, :QK_ROPE].T


def _layer(x2, batch, seq, norm_g, w_in, b_gate, lb, hg_norm_g, q_a_g, w_uq, kv_a_g, w_ukv,
           w_proj_a, w_proj_b, w_out, out_g, rope_tabs):
    half = QK_ROPE // 2
    w_in = w_in.astype(BF16)
    o = 0
    w_hq, w_hf, w_hi, w_hz = (w_in[:, o + i * HG_WIDTH:o + (i + 1) * HG_WIDTH] for i in range(4))
    o += 4 * HG_WIDTH
    w_cq = w_in[:, o:o + Q_LORA]; o += Q_LORA
    w_ckv = w_in[:, o:o + KV_LORA]; o += KV_LORA
    w_kr = w_in[:, o:o + QK_ROPE]; o += QK_ROPE
    w_mz = w_in[:, o:o + MLA_WIDTH]; o += MLA_WIDTH
    w_gl = w_in[:, o:]
    kr_pad = jnp.zeros((D_MODEL, LANES - QK_ROPE), BF16)
    w_kr_swapped = jnp.concatenate([w_kr[:, half:], w_kr[:, :half]], axis=1)
    w_main = jnp.concatenate([w_hq, w_hi, w_hz, w_mz, w_gl], axis=1)
    w_mla = jnp.concatenate([w_cq, w_ckv, w_kr, kr_pad, w_kr_swapped, kr_pad], axis=1)

    uq = w_uq.reshape(Q_LORA, MLA_HEADS, QK_DIM)
    q1, q2 = uq[:, :, QK_NOPE:QK_NOPE + half], uq[:, :, QK_NOPE + half:]
    wqa = uq.reshape(Q_LORA, MLA_HEADS * QK_DIM).T.astype(BF16)
    wqb = jnp.concatenate([q2, q1], axis=-1).reshape(Q_LORA, MLA_HEADS * QK_ROPE).T.astype(BF16)
    ukv = w_ukv.reshape(KV_LORA, MLA_HEADS, QK_NOPE + V_DIM)
    wkn = ukv[:, :, :QK_NOPE].reshape(KV_LORA, MLA_HEADS * QK_NOPE).astype(BF16)
    wv = ukv[:, :, QK_NOPE:].reshape(KV_LORA, MLA_WIDTH).T.astype(BF16)

    main, lf_hi, lf_lo, k_gate, mla = _proj(x2, norm_g[None], lb[None], w_main, w_hf, w_mla)
    y_a = _hgrn(main, lf_hi, lf_lo, k_gate, hg_norm_g[None], batch, seq)
    q_scale = QK_DIM ** -0.5 * math.log2(math.e)
    qt, k_nope, k_pe, vt = _mla_up(mla, q_a_g[None], kv_a_g[None], wqa, wqb, wkn, wv, rope_tabs,
                                   batch, seq, q_scale)
    y_b = _attn(qt, k_nope, k_pe, vt, main, batch, seq)
    return _merge(x2, y_a, y_b, main, b_gate[None], w_proj_a.astype(BF16),
                  w_proj_b.astype(BF16), w_out.astype(BF16), out_g[None])


def kernel(x, norm_g, w_in, b_gate, lb_logits, hg_norm_g, q_a_g, w_uq, kv_a_g, w_ukv,
           w_proj_a, w_proj_b, w_out, final_norm_g):
    batch, seq, _ = x.shape
    depth = norm_g.shape[0]
    assert depth == 1, "the final RMSNorm is fused into the single layer's merge kernel"
    lower_bounds = jnp.cumsum(jax.nn.softmax(lb_logits.astype(F32), axis=0), axis=0)[:depth]
    rope_tabs = _rope_tables(seq)
    x2 = x.reshape(batch * seq, D_MODEL)
    out = _layer(x2, batch, seq, norm_g[0], w_in[0], b_gate[0], lower_bounds[0], hg_norm_g[0],
                 q_a_g[0], w_uq[0], kv_a_g[0], w_ukv[0], w_proj_a[0], w_proj_b[0], w_out[0],
                 final_norm_g, rope_tabs)
    return out.reshape(batch, seq, D_MODEL)
```

```python
import functools
import math

import jax
import jax.numpy as jnp
from jax import lax
from jax.experimental import pallas as pl
from jax.experimental.pallas import tpu as pltpu

F32 = jnp.float32
BF16 = jnp.bfloat16

D_MODEL = 1024
HG_HEADS = 8
HG_DIM = 128
HG_WIDTH = HG_HEADS * HG_DIM
HG_CHUNK = 32
MLA_HEADS = 8
QK_NOPE = 128
QK_ROPE = 64
QK_DIM = QK_NOPE + QK_ROPE
V_DIM = 128
Q_LORA = 3 * D_MODEL // 8
KV_LORA = D_MODEL // 4
MLA_WIDTH = MLA_HEADS * V_DIM
ROPE_THETA = 10000.0
EPS = 1e-6

LANES = 128
QK_PAD = 2 * LANES
HG_GROUP = 128
VMEM_LIMIT = 56 * 1024 * 1024

PROJ_TM = 512
PROJ_CW = 512
HG_TC = 512
MLA_TS = 512
ATT_TQ = 4096
ATT_TK = 512
ATT_LOOKAHEAD = 2
ATT_SLOTS = ATT_TQ // ATT_TK
ATT_UNROLL = 2
BF16_ROWS = 16
V_EXT = V_DIM + BF16_ROWS
MERGE_TM = 1024
MERGE_SUB = 512

MAIN_SECTIONS = ("q", "v", "z_a", "z_b", "gates", "gates")
MAIN_W = len(MAIN_SECTIONS) * D_MODEL
MLA_W = Q_LORA + KV_LORA + 2 * LANES


def _resident(shape):
    return pl.BlockSpec(shape, lambda *_: (0,) * len(shape), pipeline_mode=pl.Buffered(1))


def _rms(x, g):
    return x * lax.rsqrt(jnp.mean(x * x, axis=-1, keepdims=True) + EPS) * g


def _dot(a, b):
    return jnp.dot(a, b, preferred_element_type=F32)


def _dot_nt(a, b):
    return lax.dot_general(a, b, (((1,), (1,)), ((), ())), preferred_element_type=F32)


def _sigmoid(x):
    return 0.5 * jnp.tanh(0.5 * x) + 0.5


def _silu(x):
    half = 0.5 * x
    return half + half * jnp.tanh(half)


def _proj_body(x_ref, g_ref, lb_ref, wmain_ref, whf_ref, wmla_ref,
               main_ref, lfh_ref, lfl_ref, k_ref, mla_ref):
    h = _rms(x_ref[...], g_ref[...]).astype(BF16)

    def main_chunk(c):
        cs = slice(c * PROJ_CW, (c + 1) * PROJ_CW)
        y = _dot(h, wmain_ref[:, cs])
        if MAIN_SECTIONS[c * PROJ_CW // D_MODEL] in ("q", "z_a"):
            y = _silu(y)
        main_ref[:, cs] = y.astype(BF16)

    def forget_chunk(c):
        cs = slice(c * PROJ_CW, (c + 1) * PROJ_CW)
        lb = lb_ref[:, cs]
        span = 0.5 * (1.0 - lb)
        mid = 0.5 * (1.0 + lb)
        span_t = span * jnp.tanh(0.5 * _dot(h, whf_ref[:, cs]))
        f = mid + span_t
        log_f = jnp.log2(f)
        lf_hi = log_f.astype(BF16)
        lfh_ref[:, cs] = lf_hi
        lfl_ref[:, cs] = (log_f - lf_hi.astype(F32)).astype(BF16)
        k_ref[:, cs] = (span - span_t).astype(BF16)

    plain = [c for c in range(MAIN_W // PROJ_CW)
             if MAIN_SECTIONS[c * PROJ_CW // D_MODEL] not in ("q", "z_a")]
    heavy = ([("forget", c) for c in range(HG_WIDTH // PROJ_CW)]
             + [("main", c) for c in range(MAIN_W // PROJ_CW) if c not in plain])
    for kind, c in heavy:
        forget_chunk(c) if kind == "forget" else main_chunk(c)
        if plain:
            main_chunk(plain.pop(0))
    for c in plain:
        main_chunk(c)
    mla_ref[...] = _dot(h, wmla_ref[...]).astype(BF16)


def _proj(x2, norm_g, lb, w_main, w_hf, w_mla):
    t = x2.shape[0]
    row = lambda w: pl.BlockSpec((PROJ_TM, w), lambda i: (i, 0))
    out = lambda w: jax.ShapeDtypeStruct((t, w), BF16)
    return pl.pallas_call(
        _proj_body,
        grid=(t // PROJ_TM,),
        in_specs=[row(D_MODEL), _resident((1, D_MODEL)), _resident(lb.shape),
                  _resident(w_main.shape),
                  _resident(w_hf.shape), _resident(w_mla.shape)],
        out_specs=[row(MAIN_W), row(HG_WIDTH), row(HG_WIDTH), row(HG_WIDTH), row(MLA_W)],
        out_shape=[out(MAIN_W), out(HG_WIDTH), out(HG_WIDTH), out(HG_WIDTH), out(MLA_W)],
        compiler_params=pltpu.CompilerParams(
            dimension_semantics=("arbitrary",), vmem_limit_bytes=VMEM_LIMIT),
        name="proj",
    )(x2, norm_g, lb, w_main, w_hf, w_mla)


def _hgrn_body(q_ref, lfh_ref, lfl_ref, k_ref, v_ref, z_ref, g_ref, o_ref, st_ref):
    @pl.when(pl.program_id(1) == 0)
    def _():
        st_ref[...] = jnp.zeros_like(st_ref)

    n_chunks = HG_GROUP // HG_CHUNK
    shift = HG_CHUNK.bit_length() - 1
    r = lax.broadcasted_iota(jnp.int32, (HG_GROUP, HG_GROUP), 0)
    c = lax.broadcasted_iota(jnp.int32, (HG_GROUP, HG_GROUP), 1)
    col_chunk = lax.shift_right_logical(c, shift)
    causal = (lax.shift_right_logical(r, shift) == col_chunk) & (r >= c)
    prefix_mat = jnp.where(causal, 1.0, 0.0).astype(BF16)
    r_blk = lax.broadcasted_iota(jnp.int32, (HG_CHUNK, HG_GROUP), 0)
    c_blk = lax.broadcasted_iota(jnp.int32, (HG_CHUNK, HG_GROUP), 1)
    key_chunk_is = [lax.shift_right_logical(c_blk, shift) == kc for kc in range(n_chunks)]
    causal_blk = [key_chunk_is[qc] & (c_blk <= r_blk + qc * HG_CHUNK) for qc in range(n_chunks)]
    pairs = [(kc + gap, kc) for gap in range(1, n_chunks) for kc in range(n_chunks - gap)]

    def chunk_rows(ci):
        return slice(ci * HG_CHUNK, (ci + 1) * HG_CHUNK)

    def by_chunk(fn, x):
        return jnp.concatenate([fn(ci, x[chunk_rows(ci)]) for ci in range(n_chunks)], axis=0)

    def group(gi, carry):
        rows = pl.ds(pl.multiple_of(gi * HG_GROUP, HG_GROUP), HG_GROUP)
        b_all = (_dot(prefix_mat, lfh_ref[rows, :])
                 + _dot(prefix_mat, lfl_ref[rows, :]))

        partial = []
        for h in range(HG_HEADS):
            cols = slice(h * HG_DIM, (h + 1) * HG_DIM)
            b = b_all[:, cols]
            k = k_ref[rows, cols].astype(F32)
            tot = [b[(ci + 1) * HG_CHUNK - 1:(ci + 1) * HG_CHUNK] for ci in range(n_chunks)]
            zero = jnp.zeros_like(tot[0])
            before = [zero]
            for ci in range(1, n_chunks):
                before.append(before[-1] + tot[ci - 1])
            after = [zero]
            for ci in range(n_chunks - 2, -1, -1):
                after.insert(0, after[0] + tot[ci + 1])
            q_in = q_ref[rows, cols].astype(F32) * jnp.exp2(b)
            k_in = k * jnp.exp2(-b)
            k_out = by_chunk(lambda ci, x: x * jnp.exp2(tot[ci]), k_in).astype(BF16)
            q_start = by_chunk(lambda ci, x: x * jnp.exp2(before[ci]), q_in).astype(BF16)
            k_end = by_chunk(lambda ci, x: x * jnp.exp2(tot[ci] + after[ci]), k_in).astype(BF16)
            k_in = k_in.astype(BF16)
            q_cross = []
            for qc, kc in pairs:
                q_blk = q_in[chunk_rows(qc)]
                if qc > kc + 1:
                    q_blk = q_blk * jnp.exp2(before[qc] - before[kc + 1])
                q_cross.append(q_blk)
            q_cross = jnp.concatenate(q_cross, axis=0).astype(BF16)

            v = v_ref[rows, cols]
            v_t = v.T
            st = st_ref[h]
            same = _dot_nt(q_in.astype(BF16), k_in)
            cross = _dot_nt(q_cross, k_out)
            o_start = _dot_nt(q_start, st.astype(BF16))
            st_ref[h] = st * jnp.exp2(before[-1] + tot[-1]) + _dot(v_t, k_end)
            partial.append((same, cross, o_start, v))

        for h in range(HG_HEADS):
            cols = slice(h * HG_DIM, (h + 1) * HG_DIM)
            same, cross, o_start, v = partial[h]
            score_rows = []
            for qc in range(n_chunks):
                blk = jnp.where(causal_blk[qc], same[chunk_rows(qc)], 0.0)
                for idx, (pq, pk) in enumerate(pairs):
                    if pq == qc:
                        blk = jnp.where(key_chunk_is[pk], cross[chunk_rows(idx)], blk)
                score_rows.append(blk)
            scores = jnp.concatenate(score_rows, axis=0).astype(BF16)
            o = _dot(scores, v) + o_start
            y = _rms(o, g_ref[...]) * z_ref[rows, cols].astype(F32)
            o_ref[rows, cols] = y.astype(BF16)
        return carry

    lax.fori_loop(0, HG_TC // HG_GROUP, group, 0, unroll=True)


def _hgrn(main, lf_hi, lf_lo, k, hg_g, batch, seq):
    nblk = seq // HG_TC
    t = batch * seq

    def col(name=None):
        j = 0 if name is None else MAIN_SECTIONS.index(name)
        return pl.BlockSpec((HG_TC, HG_WIDTH), lambda b, s: (b * nblk + s, j))

    return pl.pallas_call(
        _hgrn_body,
        grid=(batch, nblk),
        in_specs=[col("q"), col(), col(), col(), col("v"), col("z_a"), _resident((1, HG_DIM))],
        out_specs=col(),
        out_shape=jax.ShapeDtypeStruct((t, HG_WIDTH), BF16),
        scratch_shapes=[pltpu.VMEM((HG_HEADS, HG_DIM, HG_DIM), F32)],
        compiler_params=pltpu.CompilerParams(
            dimension_semantics=("arbitrary", "arbitrary"), vmem_limit_bytes=VMEM_LIMIT),
        name="hgrn2",
    )(main, lf_hi, lf_lo, k, main, main, hg_g)


def _mla_up_body(mla_ref, qg_ref, kvg_ref, wqa_ref, wqb_ref, wkn_ref, wv_ref,
                 cos_ref, sin_ref, cos_t_ref, sin_t_ref, qt_ref, kn_ref, kpe_ref, vt_ref,
                 *, q_scale):
    lat = mla_ref[...].astype(F32)
    cq = _rms(lat[:, :Q_LORA], qg_ref[...]).astype(BF16)
    ckv = _rms(lat[:, Q_LORA:Q_LORA + KV_LORA], kvg_ref[...]).astype(BF16)
    kr_a = lat[:, Q_LORA + KV_LORA:Q_LORA + KV_LORA + LANES]
    kr_b = lat[:, Q_LORA + KV_LORA + LANES:]
    kpe_ref[...] = (kr_a * cos_ref[...] + kr_b * sin_ref[...]).astype(BF16)
    k_nope = _dot(ckv, wkn_ref[...])
    qa_t = _dot_nt(wqa_ref[...], cq)
    qb_t = _dot_nt(wqb_ref[...], cq)
    v_t = _dot_nt(wv_ref[...], ckv)
    cos_t = cos_t_ref[...]
    sin_t = sin_t_ref[...]
    ones_row = jnp.where(lax.broadcasted_iota(jnp.int32, (BF16_ROWS, ATT_TK), 0) == 0,
                         1.0, 0.0).astype(BF16)
    q_zero = jnp.zeros((QK_PAD - QK_DIM, MLA_TS), BF16)
    for h in range(MLA_HEADS):
        hs = slice(h * LANES, (h + 1) * LANES)
        q_h = qa_t[h * QK_DIM:(h + 1) * QK_DIM]
        qt_ref[h, :QK_NOPE, :] = (q_h[:QK_NOPE] * q_scale).astype(BF16)
        q_pe = q_h[QK_NOPE:] * cos_t + qb_t[h * QK_ROPE:(h + 1) * QK_ROPE] * sin_t
        qt_ref[h, QK_NOPE:QK_DIM, :] = (q_pe * q_scale).astype(BF16)
        qt_ref[h, QK_DIM:, :] = q_zero
        kn_ref[h] = k_nope[:, hs].astype(BF16)
        for c in range(MLA_TS // ATT_TK):
            vt_ref[h, c, :V_DIM, :] = v_t[hs, c * ATT_TK:(c + 1) * ATT_TK].astype(BF16)
            vt_ref[h, c, V_DIM:, :] = ones_row


def _mla_up(mla, qg, kvg, wqa, wqb, wkn, wv, tabs, batch, seq, q_scale):
    nblk = seq // MLA_TS
    kv_per_step = MLA_TS // ATT_TK
    tab = pl.BlockSpec((MLA_TS, LANES), lambda b, s: (s, 0))
    tab_t = pl.BlockSpec((QK_ROPE, MLA_TS), lambda b, s: (0, s))
    return pl.pallas_call(
        functools.partial(_mla_up_body, q_scale=q_scale),
        grid=(batch, nblk),
        in_specs=[pl.BlockSpec((MLA_TS, MLA_W), lambda b, s: (b * nblk + s, 0)),
                  _resident(qg.shape), _resident(kvg.shape), _resident(wqa.shape),
                  _resident(wqb.shape), _resident(wkn.shape), _resident(wv.shape),
                  tab, tab, tab_t, tab_t],
        out_specs=[pl.BlockSpec((MLA_HEADS, QK_PAD, MLA_TS), lambda b, s: (b, 0, s)),
                   pl.BlockSpec((MLA_HEADS, MLA_TS, QK_NOPE), lambda b, s: (b, s, 0)),
                   pl.BlockSpec((MLA_TS, LANES), lambda b, s: (b * nblk + s, 0)),
                   pl.BlockSpec((MLA_HEADS, kv_per_step, V_EXT, ATT_TK),
                                lambda b, s: (b, s, 0, 0))],
        out_shape=[jax.ShapeDtypeStruct((batch * MLA_HEADS, QK_PAD, seq), BF16),
                   jax.ShapeDtypeStruct((batch * MLA_HEADS, seq, QK_NOPE), BF16),
                   jax.ShapeDtypeStruct((batch * seq, LANES), BF16),
                   jax.ShapeDtypeStruct((batch * MLA_HEADS, seq // ATT_TK, V_EXT, ATT_TK), BF16)],
        compiler_params=pltpu.CompilerParams(
            dimension_semantics=("arbitrary", "arbitrary"), vmem_limit_bytes=VMEM_LIMIT),
        name="mla_up",
    )(mla, qg, kvg, wqa, wqb, wkn, wv, *tabs)


def _attn_body(qt_ref, kn_ref, kpe_ref, vt_ref, z_ref, o_ref, s_ref, smax_ref, acc_ref, m_ref):
    qi = pl.program_id(2)
    n_strips = ATT_TQ // ATT_TK

    def lanes_of(si):
        return slice(si * ATT_TK, (si + 1) * ATT_TK)

    def scores(item, slot):
        j, si, mask = item
        ks = pl.ds(pl.multiple_of(j * ATT_TK, ATT_TK), ATT_TK)
        keys = jnp.concatenate([kn_ref[0, ks, :], kpe_ref[ks, :]], axis=1)
        s = _dot(keys, qt_ref[0, :, lanes_of(si)])
        if mask is not None:
            key = lax.broadcasted_iota(jnp.int32, s.shape, 0)
            query = lax.broadcasted_iota(jnp.int32, s.shape, 1)
            if mask == "positions":
                query = query + ((qi * n_strips + si - j) * ATT_TK)
            s = jnp.where(query >= key, s, -jnp.inf)
        s_ref[slot] = s
        smax_ref[slot] = jnp.max(s, axis=0, keepdims=True)

    def softmax_pv(item, slot):
        j, si, _ = item
        lanes = lanes_of(si)
        m_prev = m_ref[:, lanes]
        m_new = jnp.maximum(m_prev, smax_ref[slot])
        alpha = jnp.exp2(m_prev - m_new)
        p = jnp.exp2(s_ref[slot] - m_new).astype(BF16)
        acc_ref[:, lanes] = alpha * acc_ref[:, lanes] + _dot(vt_ref[0, j], p)
        m_ref[:, lanes] = m_new

    def run(items, n_consume):
        for idx in range(n_consume):
            ahead = idx + ATT_LOOKAHEAD
            if ahead < len(items):
                scores(items[ahead], ahead % ATT_SLOTS)
            softmax_pv(items[idx], idx % ATT_SLOTS)
            if items[idx][2] == "diagonal":
                finish_strip(items[idx][1])

    def finish_strip(si):
        lanes = lanes_of(si)
        o_t = acc_ref[:V_DIM, lanes] * (1.0 / acc_ref[V_DIM:V_DIM + 1, lanes])
        o_ref[lanes, :] = (o_t.T * _silu(z_ref[lanes, :].astype(F32))).astype(BF16)

    def key_blocks(t, carry):
        j = t * ATT_UNROLL
        items = [(j + b, si, None) for b in range(ATT_UNROLL) for si in range(n_strips)]
        items += [(j + ATT_UNROLL, si, "positions" if si == 0 else None)
                  for si in range(n_strips)]
        run(items, ATT_UNROLL * n_strips)
        return carry

    first = qi * n_strips
    diagonal = [(first + c, si, "diagonal" if si == c else None)
                for c in range(n_strips) for si in range(c, n_strips)]

    acc_ref[...] = jnp.zeros_like(acc_ref)
    m_ref[...] = jnp.full_like(m_ref, -jnp.inf)
    for idx in range(ATT_LOOKAHEAD):
        scores((0, idx, "positions" if idx == 0 else None), idx)
    lax.fori_loop(0, qi * (n_strips // ATT_UNROLL), key_blocks, 0)
    run(diagonal, len(diagonal))


def _attn(qt, k_nope, k_pe, vt, main, batch, seq):
    assert ATT_TQ % ATT_TK == 0
    nq = seq // ATT_TQ
    z_col0 = MAIN_SECTIONS.index("z_b") * D_MODEL // V_DIM
    head = lambda b, h, i: b * MLA_HEADS + h
    return pl.pallas_call(
        _attn_body,
        grid=(batch, MLA_HEADS, nq),
        in_specs=[pl.BlockSpec((1, QK_PAD, ATT_TQ), lambda b, h, i: (head(b, h, i), 0, i)),
                  pl.BlockSpec((1, seq, QK_NOPE), lambda b, h, i: (head(b, h, i), 0, 0)),
                  pl.BlockSpec((seq, LANES), lambda b, h, i: (b, 0)),
                  pl.BlockSpec((1, seq // ATT_TK, V_EXT, ATT_TK),
                               lambda b, h, i: (head(b, h, i), 0, 0, 0)),
                  pl.BlockSpec((ATT_TQ, V_DIM), lambda b, h, i: (b * nq + i, z_col0 + h))],
        out_specs=pl.BlockSpec((ATT_TQ, V_DIM), lambda b, h, i: (b * nq + i, h)),
        out_shape=jax.ShapeDtypeStruct((batch * seq, MLA_WIDTH), BF16),
        scratch_shapes=[pltpu.VMEM((ATT_SLOTS, ATT_TK, ATT_TK), F32),
                        pltpu.VMEM((ATT_SLOTS, 1, ATT_TK), F32),
                        pltpu.VMEM((V_EXT, ATT_TQ), F32), pltpu.VMEM((1, ATT_TQ), F32)],
        compiler_params=pltpu.CompilerParams(
            dimension_semantics=("arbitrary", "arbitrary", "arbitrary"),
            vmem_limit_bytes=VMEM_LIMIT),
        name="attn",
    )(qt, k_nope, k_pe, vt, main)


def _merge_body(x_ref, ya_ref, yb_ref, ga_ref, gb_ref, bg_ref, wpa_ref, wpb_ref, wout_ref,
                fg_ref, o_ref):
    subs = [slice(r, r + MERGE_SUB) for r in range(0, MERGE_TM, MERGE_SUB)]
    branches = [(_dot(ya_ref[rs, :], wpa_ref[...]), _dot(yb_ref[rs, :], wpb_ref[...]))
                for rs in subs]
    for rs, (pa, pb) in zip(subs, branches):
        gate_a = _sigmoid(ga_ref[rs, :].astype(F32) + bg_ref[:, :D_MODEL])
        gate_b = _sigmoid(gb_ref[rs, :].astype(F32) + bg_ref[:, D_MODEL:])
        merged = (gate_a * pa + gate_b * pb).astype(BF16)
        x_new = x_ref[rs, :] + _dot(merged, wout_ref[...])
        o_ref[rs, :] = _rms(x_new, fg_ref[...])


def _merge(x2, ya, yb, main, b_gate, wpa, wpb, wout, fg):
    t = x2.shape[0]
    gates = MAIN_SECTIONS.index("gates")
    row = lambda j: pl.BlockSpec((MERGE_TM, D_MODEL), lambda i: (i, j))
    return pl.pallas_call(
        _merge_body,
        grid=(t // MERGE_TM,),
        in_specs=[row(0), row(0), row(0), row(gates), row(gates + 1), _resident(b_gate.shape),
                  _resident(wpa.shape), _resident(wpb.shape), _resident(wout.shape),
                  _resident(fg.shape)],
        out_specs=row(0),
        out_shape=jax.ShapeDtypeStruct((t, D_MODEL), F32),
        compiler_params=pltpu.CompilerParams(
            dimension_semantics=("arbitrary",), vmem_limit_bytes=VMEM_LIMIT),
        name="merge",
    )(x2, ya, yb, main, main, b_gate, wpa, wpb, wout, fg)


def _rope_tables(seq):
    inv = ROPE_THETA ** (-jnp.arange(0, QK_ROPE, 2, dtype=F32) / QK_ROPE)
    ang = jnp.arange(seq, dtype=F32)[:, None] * inv[None, :]
    cos, sin = jnp.cos(ang), jnp.sin(ang)
    zero = jnp.zeros_like(cos)
    cos_tab = jnp.concatenate([cos, cos, zero, zero], axis=-1)
    sin_tab = jnp.concatenate([-sin, sin, zero, zero], axis=-1)
    return cos_tab, sin_tab, cos_tab[:, :QK_ROPE].T, sin_tab[:, :QK_ROPE].T


def _layer(x2, batch, seq, norm_g, w_in, b_gate, lb, hg_norm_g, q_a_g, w_uq, kv_a_g, w_ukv,
           w_proj_a, w_proj_b, w_out, out_g, rope_tabs):
    half = QK_ROPE // 2
    w_in = w_in.astype(BF16)
    o = 0
    w_hq, w_hf, w_hi, w_hz = (w_in[:, o + i * HG_WIDTH:o + (i + 1) * HG_WIDTH] for i in range(4))
    o += 4 * HG_WIDTH
    w_cq = w_in[:, o:o + Q_LORA]; o += Q_LORA
    w_ckv = w_in[:, o:o + KV_LORA]; o += KV_LORA
    w_kr = w_in[:, o:o + QK_ROPE]; o += QK_ROPE
    w_mz = w_in[:, o:o + MLA_WIDTH]; o += MLA_WIDTH
    w_gl = w_in[:, o:]
    kr_pad = jnp.zeros((D_MODEL, LANES - QK_ROPE), BF16)
    w_kr_swapped = jnp.concatenate([w_kr[:, half:], w_kr[:, :half]], axis=1)
    w_main = jnp.concatenate([w_hq, w_hi, w_hz, w_mz, w_gl], axis=1)
    w_mla = jnp.concatenate([w_cq, w_ckv, w_kr, kr_pad, w_kr_swapped, kr_pad], axis=1)

    uq = w_uq.reshape(Q_LORA, MLA_HEADS, QK_DIM)
    q1, q2 = uq[:, :, QK_NOPE:QK_NOPE + half], uq[:, :, QK_NOPE + half:]
    wqa = uq.reshape(Q_LORA, MLA_HEADS * QK_DIM).T.astype(BF16)
    wqb = jnp.concatenate([q2, q1], axis=-1).reshape(Q_LORA, MLA_HEADS * QK_ROPE).T.astype(BF16)
    ukv = w_ukv.reshape(KV_LORA, MLA_HEADS, QK_NOPE + V_DIM)
    wkn = ukv[:, :, :QK_NOPE].reshape(KV_LORA, MLA_HEADS * QK_NOPE).astype(BF16)
    wv = ukv[:, :, QK_NOPE:].reshape(KV_LORA, MLA_WIDTH).T.astype(BF16)

    main, lf_hi, lf_lo, k_gate, mla = _proj(x2, norm_g[None], lb[None], w_main, w_hf, w_mla)
    y_a = _hgrn(main, lf_hi, lf_lo, k_gate, hg_norm_g[None], batch, seq)
    q_scale = QK_DIM ** -0.5 * math.log2(math.e)
    qt, k_nope, k_pe, vt = _mla_up(mla, q_a_g[None], kv_a_g[None], wqa, wqb, wkn, wv, rope_tabs,
                                   batch, seq, q_scale)
    y_b = _attn(qt, k_nope, k_pe, vt, main, batch, seq)
    return _merge(x2, y_a, y_b, main, b_gate[None], w_proj_a.astype(BF16),
                  w_proj_b.astype(BF16), w_out.astype(BF16), out_g[None])


def kernel(x, norm_g, w_in, b_gate, lb_logits, hg_norm_g, q_a_g, w_uq, kv_a_g, w_ukv,
           w_proj_a, w_proj_b, w_out, final_norm_g):
    batch, seq, _ = x.shape
    depth = norm_g.shape[0]
    assert depth == 1, "the final RMSNorm is fused into the single layer's merge kernel"
    lower_bounds = jnp.cumsum(jax.nn.softmax(lb_logits.astype(F32), axis=0), axis=0)[:depth]
    rope_tabs = _rope_tables(seq)
    x2 = x.reshape(batch * seq, D_MODEL)
    out = _layer(x2, batch, seq, norm_g[0], w_in[0], b_gate[0], lower_bounds[0], hg_norm_g[0],
                 q_a_g[0], w_uq[0], kv_a_g[0], w_ukv[0], w_proj_a[0], w_proj_b[0], w_out[0],
                 final_norm_g, rope_tabs)
    return out.reshape(batch, seq, D_MODEL)
```

```python
import functools
import math

import jax
import jax.numpy as jnp
from jax import lax
from jax.experimental import pallas as pl
from jax.experimental.pallas import tpu as pltpu

F32 = jnp.float32
BF16 = jnp.bfloat16

D_MODEL = 1024
HG_HEADS = 8
HG_DIM = 128
HG_WIDTH = HG_HEADS * HG_DIM
HG_CHUNK = 32
MLA_HEADS = 8
QK_NOPE = 128
QK_ROPE = 64
QK_DIM = QK_NOPE + QK_ROPE
V_DIM = 128
Q_LORA = 3 * D_MODEL // 8
KV_LORA = D_MODEL // 4
MLA_WIDTH = MLA_HEADS * V_DIM
ROPE_THETA = 10000.0
EPS = 1e-6

LANES = 128
QK_PAD = 2 * LANES
HG_GROUP = 128
VMEM_LIMIT = 56 * 1024 * 1024

PROJ_TM = 512
PROJ_CW = 512
HG_TC = 512
MLA_TS = 512
ATT_TQ = 4096
ATT_TK = 512
ATT_LOOKAHEAD = 2
ATT_SLOTS = ATT_TQ // ATT_TK
ATT_UNROLL = 2
BF16_ROWS = 16
V_EXT = V_DIM + BF16_ROWS
MERGE_TM = 1024
MERGE_SUB = 512

MAIN_SECTIONS = ("q", "v", "z_a", "z_b", "gates", "gates")
MAIN_W = len(MAIN_SECTIONS) * D_MODEL
MLA_W = Q_LORA + KV_LORA + 2 * LANES


def _resident(shape):
    return pl.BlockSpec(shape, lambda *_: (0,) * len(shape), pipeline_mode=pl.Buffered(1))


def _rms(x, g):
    return x * lax.rsqrt(jnp.mean(x * x, axis=-1, keepdims=True) + EPS) * g


def _dot(a, b):
    return jnp.dot(a, b, preferred_element_type=F32)


def _dot_nt(a, b):
    return lax.dot_general(a, b, (((1,), (1,)), ((), ())), preferred_element_type=F32)


def _sigmoid(x):
    return 0.5 * jnp.tanh(0.5 * x) + 0.5


def _silu(x):
    half = 0.5 * x
    return half + half * jnp.tanh(half)


def _proj_body(x_ref, g_ref, lb_ref, wmain_ref, whf_ref, wmla_ref,
               main_ref, lfh_ref, lfl_ref, k_ref, mla_ref):
    h = _rms(x_ref[...], g_ref[...]).astype(BF16)

    def main_chunk(c):
        cs = slice(c * PROJ_CW, (c + 1) * PROJ_CW)
        y = _dot(h, wmain_ref[:, cs])
        if MAIN_SECTIONS[c * PROJ_CW // D_MODEL] in ("q", "z_a"):
            y = _silu(y)
        main_ref[:, cs] = y.astype(BF16)

    def forget_chunk(c):
        half = PROJ_CW // 2
        y = _dot(h, whf_ref[:, c * PROJ_CW:(c + 1) * PROJ_CW])
        v_col = MAIN_SECTIONS.index("v") * D_MODEL + c * half
        main_ref[:, v_col:v_col + half] = y[:, half:].astype(BF16)
        cs = slice(c * half, (c + 1) * half)
        lb = lb_ref[:, cs]
        span = 0.5 * (1.0 - lb)
        mid = 0.5 * (1.0 + lb)
        span_t = span * jnp.tanh(0.5 * y[:, :half])
        f = mid + span_t
        log_f = jnp.log2(f)
        lf_hi = log_f.astype(BF16)
        lfh_ref[:, cs] = lf_hi
        lfl_ref[:, cs] = (log_f - lf_hi.astype(F32)).astype(BF16)
        k_ref[:, cs] = (span - span_t).astype(BF16)

    section_of = lambda c: MAIN_SECTIONS[c * PROJ_CW // D_MODEL]
    plain = [c for c in range(MAIN_W // PROJ_CW) if section_of(c) not in ("q", "z_a", "v")]
    heavy = ([("forget", c) for c in range(2 * HG_WIDTH // PROJ_CW)]
             + [("main", c) for c in range(MAIN_W // PROJ_CW) if section_of(c) in ("q", "z_a")])
    for kind, c in heavy:
        forget_chunk(c) if kind == "forget" else main_chunk(c)
        if plain:
            main_chunk(plain.pop(0))
    for c in plain:
        main_chunk(c)
    mla_ref[...] = _dot(h, wmla_ref[...]).astype(BF16)


def _proj(x2, norm_g, lb, w_main, w_hf, w_mla):
    t = x2.shape[0]
    row = lambda w: pl.BlockSpec((PROJ_TM, w), lambda i: (i, 0))
    out = lambda w: jax.ShapeDtypeStruct((t, w), BF16)
    return pl.pallas_call(
        _proj_body,
        grid=(t // PROJ_TM,),
        in_specs=[row(D_MODEL), _resident((1, D_MODEL)), _resident(lb.shape),
                  _resident(w_main.shape),
                  _resident(w_hf.shape), _resident(w_mla.shape)],
        out_specs=[row(MAIN_W), row(HG_WIDTH), row(HG_WIDTH), row(HG_WIDTH), row(MLA_W)],
        out_shape=[out(MAIN_W), out(HG_WIDTH), out(HG_WIDTH), out(HG_WIDTH), out(MLA_W)],
        compiler_params=pltpu.CompilerParams(
            dimension_semantics=("arbitrary",), vmem_limit_bytes=VMEM_LIMIT),
        name="proj",
    )(x2, norm_g, lb, w_main, w_hf, w_mla)


def _hgrn_body(q_ref, lfh_ref, lfl_ref, k_ref, v_ref, z_ref, g_ref, o_ref, st_ref):
    @pl.when(pl.program_id(1) == 0)
    def _():
        st_ref[...] = jnp.zeros_like(st_ref)

    n_chunks = HG_GROUP // HG_CHUNK
    shift = HG_CHUNK.bit_length() - 1
    r = lax.broadcasted_iota(jnp.int32, (HG_GROUP, HG_GROUP), 0)
    c = lax.broadcasted_iota(jnp.int32, (HG_GROUP, HG_GROUP), 1)
    col_chunk = lax.shift_right_logical(c, shift)
    causal = (lax.shift_right_logical(r, shift) == col_chunk) & (r >= c)
    prefix_mat = jnp.where(causal, 1.0, 0.0).astype(BF16)
    r_blk = lax.broadcasted_iota(jnp.int32, (HG_CHUNK, HG_GROUP), 0)
    c_blk = lax.broadcasted_iota(jnp.int32, (HG_CHUNK, HG_GROUP), 1)
    key_chunk_is = [lax.shift_right_logical(c_blk, shift) == kc for kc in range(n_chunks)]
    causal_blk = [key_chunk_is[qc] & (c_blk <= r_blk + qc * HG_CHUNK) for qc in range(n_chunks)]
    pairs = [(kc + gap, kc) for gap in range(1, n_chunks) for kc in range(n_chunks - gap)]

    def chunk_rows(ci):
        return slice(ci * HG_CHUNK, (ci + 1) * HG_CHUNK)

    def by_chunk(fn, x):
        return jnp.concatenate([fn(ci, x[chunk_rows(ci)]) for ci in range(n_chunks)], axis=0)

    def group(gi, carry):
        rows = pl.ds(pl.multiple_of(gi * HG_GROUP, HG_GROUP), HG_GROUP)
        b_all = (_dot(prefix_mat, lfh_ref[rows, :])
                 + _dot(prefix_mat, lfl_ref[rows, :]))

        partial = []
        for h in range(HG_HEADS):
            cols = slice(h * HG_DIM, (h + 1) * HG_DIM)
            b = b_all[:, cols]
            k = k_ref[rows, cols].astype(F32)
            tot = [b[(ci + 1) * HG_CHUNK - 1:(ci + 1) * HG_CHUNK] for ci in range(n_chunks)]
            zero = jnp.zeros_like(tot[0])
            before = [zero]
            for ci in range(1, n_chunks):
                before.append(before[-1] + tot[ci - 1])
            after = [zero]
            for ci in range(n_chunks - 2, -1, -1):
                after.insert(0, after[0] + tot[ci + 1])
            q_in = q_ref[rows, cols].astype(F32) * jnp.exp2(b)
            k_in = k * jnp.exp2(-b)
            k_out = by_chunk(lambda ci, x: x * jnp.exp2(tot[ci]), k_in).astype(BF16)
            q_start = by_chunk(lambda ci, x: x * jnp.exp2(before[ci]), q_in).astype(BF16)
            k_end = by_chunk(lambda ci, x: x * jnp.exp2(tot[ci] + after[ci]), k_in).astype(BF16)
            k_in = k_in.astype(BF16)
            q_cross = []
            for qc, kc in pairs:
                q_blk = q_in[chunk_rows(qc)]
                if qc > kc + 1:
                    q_blk = q_blk * jnp.exp2(before[qc] - before[kc + 1])
                q_cross.append(q_blk)
            q_cross = jnp.concatenate(q_cross, axis=0).astype(BF16)

            v = v_ref[rows, cols]
            v_t = v.T
            st = st_ref[h]
            same = _dot_nt(q_in.astype(BF16), k_in)
            cross = _dot_nt(q_cross, k_out)
            o_start = _dot_nt(q_start, st.astype(BF16))
            st_ref[h] = st * jnp.exp2(before[-1] + tot[-1]) + _dot(v_t, k_end)
            partial.append((same, cross, o_start, v))

        for h in range(HG_HEADS):
            cols = slice(h * HG_DIM, (h + 1) * HG_DIM)
            same, cross, o_start, v = partial[h]
            score_rows = []
            for qc in range(n_chunks):
                blk = jnp.where(causal_blk[qc], same[chunk_rows(qc)], 0.0)
                for idx, (pq, pk) in enumerate(pairs):
                    if pq == qc:
                        blk = jnp.where(key_chunk_is[pk], cross[chunk_rows(idx)], blk)
                score_rows.append(blk)
            scores = jnp.concatenate(score_rows, axis=0).astype(BF16)
            o = _dot(scores, v) + o_start
            y = _rms(o, g_ref[...]) * z_ref[rows, cols].astype(F32)
            o_ref[rows, cols] = y.astype(BF16)
        return carry

    lax.fori_loop(0, HG_TC // HG_GROUP, group, 0, unroll=True)


def _hgrn(main, lf_hi, lf_lo, k, hg_g, batch, seq):
    nblk = seq // HG_TC
    t = batch * seq

    def col(name=None):
        j = 0 if name is None else MAIN_SECTIONS.index(name)
        return pl.BlockSpec((HG_TC, HG_WIDTH), lambda b, s: (b * nblk + s, j))

    return pl.pallas_call(
        _hgrn_body,
        grid=(batch, nblk),
        in_specs=[col("q"), col(), col(), col(), col("v"), col("z_a"), _resident((1, HG_DIM))],
        out_specs=col(),
        out_shape=jax.ShapeDtypeStruct((t, HG_WIDTH), BF16),
        scratch_shapes=[pltpu.VMEM((HG_HEADS, HG_DIM, HG_DIM), F32)],
        compiler_params=pltpu.CompilerParams(
            dimension_semantics=("arbitrary", "arbitrary"), vmem_limit_bytes=VMEM_LIMIT),
        name="hgrn2",
    )(main, lf_hi, lf_lo, k, main, main, hg_g)


def _mla_up_body(mla_ref, qg_ref, kvg_ref, wqa_ref, wqb_ref, wkn_ref, wv_ref,
                 cos_ref, sin_ref, cos_t_ref, sin_t_ref, qt_ref, kn_ref, kpe_ref, vt_ref,
                 *, q_scale):
    lat = mla_ref[...].astype(F32)
    cq = _rms(lat[:, :Q_LORA], qg_ref[...]).astype(BF16)
    ckv = _rms(lat[:, Q_LORA:Q_LORA + KV_LORA], kvg_ref[...]).astype(BF16)
    kr_a = lat[:, Q_LORA + KV_LORA:Q_LORA + KV_LORA + LANES]
    kr_b = lat[:, Q_LORA + KV_LORA + LANES:]
    kpe_ref[...] = (kr_a * cos_ref[...] + kr_b * sin_ref[...]).astype(BF16)
    k_nope = _dot(ckv, wkn_ref[...])
    qa_t = _dot_nt(wqa_ref[...], cq)
    qb_t = _dot_nt(wqb_ref[...], cq)
    v_t = _dot_nt(wv_ref[...], ckv)
    cos_t = cos_t_ref[...]
    sin_t = sin_t_ref[...]
    ones_row = jnp.where(lax.broadcasted_iota(jnp.int32, (BF16_ROWS, ATT_TK), 0) == 0,
                         1.0, 0.0).astype(BF16)
    q_zero = jnp.zeros((QK_PAD - QK_DIM, MLA_TS), BF16)
    for h in range(MLA_HEADS):
        hs = slice(h * LANES, (h + 1) * LANES)
        q_h = qa_t[h * QK_DIM:(h + 1) * QK_DIM]
        qt_ref[h, :QK_NOPE, :] = (q_h[:QK_NOPE] * q_scale).astype(BF16)
        q_pe = q_h[QK_NOPE:] * cos_t + qb_t[h * QK_ROPE:(h + 1) * QK_ROPE] * sin_t
        qt_ref[h, QK_NOPE:QK_DIM, :] = (q_pe * q_scale).astype(BF16)
        qt_ref[h, QK_DIM:, :] = q_zero
        kn_ref[h] = k_nope[:, hs].astype(BF16)
        for c in range(MLA_TS // ATT_TK):
            vt_ref[h, c, :V_DIM, :] = v_t[hs, c * ATT_TK:(c + 1) * ATT_TK].astype(BF16)
            vt_ref[h, c, V_DIM:, :] = ones_row


def _mla_up(mla, qg, kvg, wqa, wqb, wkn, wv, tabs, batch, seq, q_scale):
    nblk = seq // MLA_TS
    kv_per_step = MLA_TS // ATT_TK
    tab = pl.BlockSpec((MLA_TS, LANES), lambda b, s: (s, 0))
    tab_t = pl.BlockSpec((QK_ROPE, MLA_TS), lambda b, s: (0, s))
    return pl.pallas_call(
        functools.partial(_mla_up_body, q_scale=q_scale),
        grid=(batch, nblk),
        in_specs=[pl.BlockSpec((MLA_TS, MLA_W), lambda b, s: (b * nblk + s, 0)),
                  _resident(qg.shape), _resident(kvg.shape), _resident(wqa.shape),
                  _resident(wqb.shape), _resident(wkn.shape), _resident(wv.shape),
                  tab, tab, tab_t, tab_t],
        out_specs=[pl.BlockSpec((MLA_HEADS, QK_PAD, MLA_TS), lambda b, s: (b, 0, s)),
                   pl.BlockSpec((MLA_HEADS, MLA_TS, QK_NOPE), lambda b, s: (b, s, 0)),
                   pl.BlockSpec((MLA_TS, LANES), lambda b, s: (b * nblk + s, 0)),
                   pl.BlockSpec((MLA_HEADS, kv_per_step, V_EXT, ATT_TK),
                                lambda b, s: (b, s, 0, 0))],
        out_shape=[jax.ShapeDtypeStruct((batch * MLA_HEADS, QK_PAD, seq), BF16),
                   jax.ShapeDtypeStruct((batch * MLA_HEADS, seq, QK_NOPE), BF16),
                   jax.ShapeDtypeStruct((batch * seq, LANES), BF16),
                   jax.ShapeDtypeStruct((batch * MLA_HEADS, seq // ATT_TK, V_EXT, ATT_TK), BF16)],
        compiler_params=pltpu.CompilerParams(
            dimension_semantics=("arbitrary", "arbitrary"), vmem_limit_bytes=VMEM_LIMIT),
        name="mla_up",
    )(mla, qg, kvg, wqa, wqb, wkn, wv, *tabs)


def _attn_body(qt_ref, kn_ref, kpe_ref, vt_ref, z_ref, o_ref, s_ref, smax_ref, acc_ref, m_ref):
    qi = pl.program_id(2)
    n_strips = ATT_TQ // ATT_TK

    def lanes_of(si):
        return slice(si * ATT_TK, (si + 1) * ATT_TK)

    def scores(item, slot):
        j, si, mask = item
        ks = pl.ds(pl.multiple_of(j * ATT_TK, ATT_TK), ATT_TK)
        keys = jnp.concatenate([kn_ref[0, ks, :], kpe_ref[ks, :]], axis=1)
        s = _dot(keys, qt_ref[0, :, lanes_of(si)])
        if mask is not None:
            key = lax.broadcasted_iota(jnp.int32, s.shape, 0)
            query = lax.broadcasted_iota(jnp.int32, s.shape, 1)
            if mask == "positions":
                query = query + ((qi * n_strips + si - j) * ATT_TK)
            s = jnp.where(query >= key, s, -jnp.inf)
        s_ref[slot] = s
        smax_ref[slot] = jnp.max(s, axis=0, keepdims=True)

    def softmax_pv(item, slot):
        j, si, _ = item
        lanes = lanes_of(si)
        m_prev = m_ref[:, lanes]
        m_new = jnp.maximum(m_prev, smax_ref[slot])
        alpha = jnp.exp2(m_prev - m_new)
        p = jnp.exp2(s_ref[slot] - m_new).astype(BF16)
        acc_ref[:, lanes] = alpha * acc_ref[:, lanes] + _dot(vt_ref[0, j], p)
        m_ref[:, lanes] = m_new

    def run(items, n_consume):
        for idx in range(n_consume):
            ahead = idx + ATT_LOOKAHEAD
            if ahead < len(items):
                scores(items[ahead], ahead % ATT_SLOTS)
            softmax_pv(items[idx], idx % ATT_SLOTS)
            if items[idx][2] == "diagonal":
                finish_strip(items[idx][1])

    def finish_strip(si):
        lanes = lanes_of(si)
        o_t = acc_ref[:V_DIM, lanes] * (1.0 / acc_ref[V_DIM:V_DIM + 1, lanes])
        o_ref[lanes, :] = (o_t.T * _silu(z_ref[lanes, :].astype(F32))).astype(BF16)

    def key_blocks(t, carry):
        j = t * ATT_UNROLL
        items = [(j + b, si, None) for b in range(ATT_UNROLL) for si in range(n_strips)]
        items += [(j + ATT_UNROLL, si, "positions" if si == 0 else None)
                  for si in range(n_strips)]
        run(items, ATT_UNROLL * n_strips)
        return carry

    first = qi * n_strips
    diagonal = [(first + c, si, "diagonal" if si == c else None)
                for c in range(n_strips) for si in range(c, n_strips)]

    acc_ref[...] = jnp.zeros_like(acc_ref)
    m_ref[...] = jnp.full_like(m_ref, -jnp.inf)
    for idx in range(ATT_LOOKAHEAD):
        scores((0, idx, "positions" if idx == 0 else None), idx)
    lax.fori_loop(0, qi * (n_strips // ATT_UNROLL), key_blocks, 0)
    run(diagonal, len(diagonal))


def _attn(qt, k_nope, k_pe, vt, main, batch, seq):
    assert ATT_TQ % ATT_TK == 0
    nq = seq // ATT_TQ
    z_col0 = MAIN_SECTIONS.index("z_b") * D_MODEL // V_DIM
    head = lambda b, h, i: b * MLA_HEADS + h
    return pl.pallas_call(
        _attn_body,
        grid=(batch, MLA_HEADS, nq),
        in_specs=[pl.BlockSpec((1, QK_PAD, ATT_TQ), lambda b, h, i: (head(b, h, i), 0, i)),
                  pl.BlockSpec((1, seq, QK_NOPE), lambda b, h, i: (head(b, h, i), 0, 0)),
                  pl.BlockSpec((seq, LANES), lambda b, h, i: (b, 0)),
                  pl.BlockSpec((1, seq // ATT_TK, V_EXT, ATT_TK),
                               lambda b, h, i: (head(b, h, i), 0, 0, 0)),
                  pl.BlockSpec((ATT_TQ, V_DIM), lambda b, h, i: (b * nq + i, z_col0 + h))],
        out_specs=pl.BlockSpec((ATT_TQ, V_DIM), lambda b, h, i: (b * nq + i, h)),
        out_shape=jax.ShapeDtypeStruct((batch * seq, MLA_WIDTH), BF16),
        scratch_shapes=[pltpu.VMEM((ATT_SLOTS, ATT_TK, ATT_TK), F32),
                        pltpu.VMEM((ATT_SLOTS, 1, ATT_TK), F32),
                        pltpu.VMEM((V_EXT, ATT_TQ), F32), pltpu.VMEM((1, ATT_TQ), F32)],
        compiler_params=pltpu.CompilerParams(
            dimension_semantics=("arbitrary", "arbitrary", "arbitrary"),
            vmem_limit_bytes=VMEM_LIMIT),
        name="attn",
    )(qt, k_nope, k_pe, vt, main)


def _merge_body(x_ref, ya_ref, yb_ref, ga_ref, gb_ref, bg_ref, wpa_ref, wpb_ref, wout_ref,
                fg_ref, o_ref):
    subs = [slice(r, r + MERGE_SUB) for r in range(0, MERGE_TM, MERGE_SUB)]
    branches = [(_dot(ya_ref[rs, :], wpa_ref[...]), _dot(yb_ref[rs, :], wpb_ref[...]))
                for rs in subs]
    for rs, (pa, pb) in zip(subs, branches):
        gate_a = _sigmoid(ga_ref[rs, :].astype(F32) + bg_ref[:, :D_MODEL])
        gate_b = _sigmoid(gb_ref[rs, :].astype(F32) + bg_ref[:, D_MODEL:])
        merged = (gate_a * pa + gate_b * pb).astype(BF16)
        x_new = x_ref[rs, :] + _dot(merged, wout_ref[...])
        o_ref[rs, :] = _rms(x_new, fg_ref[...])


def _merge(x2, ya, yb, main, b_gate, wpa, wpb, wout, fg):
    t = x2.shape[0]
    gates = MAIN_SECTIONS.index("gates")
    row = lambda j: pl.BlockSpec((MERGE_TM, D_MODEL), lambda i: (i, j))
    return pl.pallas_call(
        _merge_body,
        grid=(t // MERGE_TM,),
        in_specs=[row(0), row(0), row(0), row(gates), row(gates + 1), _resident(b_gate.shape),
                  _resident(wpa.shape), _resident(wpb.shape), _resident(wout.shape),
                  _resident(fg.shape)],
        out_specs=row(0),
        out_shape=jax.ShapeDtypeStruct((t, D_MODEL), F32),
        compiler_params=pltpu.CompilerParams(
            dimension_semantics=("arbitrary",), vmem_limit_bytes=VMEM_LIMIT),
        name="merge",
    )(x2, ya, yb, main, main, b_gate, wpa, wpb, wout, fg)


def _rope_tables(seq):
    inv = ROPE_THETA ** (-jnp.arange(0, QK_ROPE, 2, dtype=F32) / QK_ROPE)
    ang = jnp.arange(seq, dtype=F32)[:, None] * inv[None, :]
    cos, sin = jnp.cos(ang), jnp.sin(ang)
    zero = jnp.zeros_like(cos)
    cos_tab = jnp.concatenate([cos, cos, zero, zero], axis=-1)
    sin_tab = jnp.concatenate([-sin, sin, zero, zero], axis=-1)
    return cos_tab, sin_tab, cos_tab[:, :QK_ROPE].T, sin_tab[:, :QK_ROPE].T


def _layer(x2, batch, seq, norm_g, w_in, b_gate, lb, hg_norm_g, q_a_g, w_uq, kv_a_g, w_ukv,
           w_proj_a, w_proj_b, w_out, out_g, rope_tabs):
    half = QK_ROPE // 2
    w_in = w_in.astype(BF16)
    o = 0
    w_hq, w_hf, w_hi, w_hz = (w_in[:, o + i * HG_WIDTH:o + (i + 1) * HG_WIDTH] for i in range(4))
    o += 4 * HG_WIDTH
    w_cq = w_in[:, o:o + Q_LORA]; o += Q_LORA
    w_ckv = w_in[:, o:o + KV_LORA]; o += KV_LORA
    w_kr = w_in[:, o:o + QK_ROPE]; o += QK_ROPE
    w_mz = w_in[:, o:o + MLA_WIDTH]; o += MLA_WIDTH
    w_gl = w_in[:, o:]
    kr_pad = jnp.zeros((D_MODEL, LANES - QK_ROPE), BF16)
    w_kr_swapped = jnp.concatenate([w_kr[:, half:], w_kr[:, :half]], axis=1)
    w_main = jnp.concatenate([w_hq, w_hi, w_hz, w_mz, w_gl], axis=1)
    w_mla = jnp.concatenate([w_cq, w_ckv, w_kr, kr_pad, w_kr_swapped, kr_pad], axis=1)

    uq = w_uq.reshape(Q_LORA, MLA_HEADS, QK_DIM)
    q1, q2 = uq[:, :, QK_NOPE:QK_NOPE + half], uq[:, :, QK_NOPE + half:]
    wqa = uq.reshape(Q_LORA, MLA_HEADS * QK_DIM).T.astype(BF16)
    wqb = jnp.concatenate([q2, q1], axis=-1).reshape(Q_LORA, MLA_HEADS * QK_ROPE).T.astype(BF16)
    ukv = w_ukv.reshape(KV_LORA, MLA_HEADS, QK_NOPE + V_DIM)
    wkn = ukv[:, :, :QK_NOPE].reshape(KV_LORA, MLA_HEADS * QK_NOPE).astype(BF16)
    wv = ukv[:, :, QK_NOPE:].reshape(KV_LORA, MLA_WIDTH).T.astype(BF16)

    n_mix = 2 * HG_WIDTH // PROJ_CW
    w_mix = jnp.concatenate([w_hf.reshape(D_MODEL, n_mix, PROJ_CW // 2),
                             w_hi.reshape(D_MODEL, n_mix, PROJ_CW // 2)],
                            axis=2).reshape(D_MODEL, 2 * HG_WIDTH)
    main, lf_hi, lf_lo, k_gate, mla = _proj(x2, norm_g[None], lb[None], w_main, w_mix, w_mla)
    y_a = _hgrn(main, lf_hi, lf_lo, k_gate, hg_norm_g[None], batch, seq)
    q_scale = QK_DIM ** -0.5 * math.log2(math.e)
    qt, k_nope, k_pe, vt = _mla_up(mla, q_a_g[None], kv_a_g[None], wqa, wqb, wkn, wv, rope_tabs,
                                   batch, seq, q_scale)
    y_b = _attn(qt, k_nope, k_pe, vt, main, batch, seq)
    return _merge(x2, y_a, y_b, main, b_gate[None], w_proj_a.astype(BF16),
                  w_proj_b.astype(BF16), w_out.astype(BF16), out_g[None])


def kernel(x, norm_g, w_in, b_gate, lb_logits, hg_norm_g, q_a_g, w_uq, kv_a_g, w_ukv,
           w_proj_a, w_proj_b, w_out, final_norm_g):
    batch, seq, _ = x.shape
    depth = norm_g.shape[0]
    assert depth == 1, "the final RMSNorm is fused into the single layer's merge kernel"
    lower_bounds = jnp.cumsum(jax.nn.softmax(lb_logits.astype(F32), axis=0), axis=0)[:depth]
    rope_tabs = _rope_tables(seq)
    x2 = x.reshape(batch * seq, D_MODEL)
    out = _layer(x2, batch, seq, norm_g[0], w_in[0], b_gate[0], lower_bounds[0], hg_norm_g[0],
                 q_a_g[0], w_uq[0], kv_a_g[0], w_ukv[0], w_proj_a[0], w_proj_b[0], w_out[0],
                 final_norm_g, rope_tabs)
    return out.reshape(batch, seq, D_MODEL)
```
